```python
import math
import jax, jax.numpy as jnp
from jax import lax
import numpy as np

D_MODEL = 1024
BATCH = 8
SEQ = 2048
DEPTH = 4

N_MIXERS = 2
N_A = (DEPTH + 1) // 2
N_B = DEPTH // 2

A_HEADS = 16
A_KV_HEADS = 4
A_GROUP = A_HEADS // A_KV_HEADS
A_HEAD_DIM = 64
A_QKV = A_HEADS * A_HEAD_DIM + 2 * A_KV_HEADS * A_HEAD_DIM
WINDOW = 128
BLK = 128

B_HEADS = 16
Q_LORA = 384
KV_LORA = 256
NOPE_DIM = 64
ROPE_DIM = 32
V_DIM = 64
B_IN = Q_LORA + KV_LORA + ROPE_DIM
ROPE_THETA = 10000.0

D_FF = int(math.ceil(8 * D_MODEL / 3 / 256) * 256)

EPS = 1e-6
NEG = -1e30

kernel_name = "hybrid_swa_sink_alibi_mla_swiglu_encoder"


def _rmsnorm(x, g):
    x32 = x.astype(jnp.float32)
    y = x32 * lax.rsqrt(jnp.mean(x32 * x32, axis=-1, keepdims=True) + EPS)
    return (y * g.astype(jnp.float32)).astype(x.dtype)


def _alibi_slopes(n):
    return jnp.asarray(2.0 ** (-8.0 * np.arange(1, n + 1) / n), dtype=jnp.float32)


def _band(t, nb):
    pad = [(0, 0), (BLK, BLK)] + [(0, 0)] * (t.ndim - 2)
    tp = jnp.pad(t, pad).reshape((t.shape[0], nb + 2, BLK) + t.shape[2:])
    return jnp.concatenate([tp[:, :-2], tp[:, 1:-1], tp[:, 2:]], axis=2)


def _windowed_gqa(h, pos, w_qkv, sink, w_o):
    B, S, _ = h.shape
    nb = S // BLK
    qkv = h @ w_qkv
    q, k, v = jnp.split(qkv, [A_HEADS * A_HEAD_DIM, A_HEADS * A_HEAD_DIM + A_KV_HEADS * A_HEAD_DIM], axis=-1)
    q = q.reshape(B, nb, BLK, A_KV_HEADS, A_GROUP, A_HEAD_DIM) * (A_HEAD_DIM ** -0.5)
    kb = _band(k.reshape(B, S, A_KV_HEADS, A_HEAD_DIM), nb)
    vb = _band(v.reshape(B, S, A_KV_HEADS, A_HEAD_DIM), nb)
    qi = jnp.arange(nb)[:, None] * BLK + jnp.arange(BLK)[None, :]
    ki = (jnp.arange(nb)[:, None] - 1) * BLK + jnp.arange(3 * BLK)[None, :]
    mask = (jnp.abs(qi[:, :, None] - ki[:, None, :]) <= WINDOW) & (ki >= 0)[:, None, :] & (ki < S)[:, None, :]
    qpos = pos.reshape(B, nb, BLK)
    kpos = _band(pos, nb)
    dist = jnp.abs(qpos[:, :, :, None] - kpos[:, :, None, :]).astype(jnp.float32)
    slopes = _alibi_slopes(A_HEADS).reshape(A_KV_HEADS, A_GROUP)
    scores = jnp.einsum('bnqkgd,bnskd->bnkgqs', q, kb).astype(jnp.float32)
    scores = scores - slopes[None, None, :, :, None, None] * dist[:, :, None, None]
    scores = jnp.where(mask[None, :, None, None], scores, NEG)
    sink_col = jnp.broadcast_to(sink.astype(jnp.float32).reshape(1, 1, A_KV_HEADS, A_GROUP, 1, 1),
                                scores.shape[:-1] + (1,))
    p = jax.nn.softmax(jnp.concatenate([scores, sink_col], axis=-1), axis=-1)[..., :-1]
    out = jnp.einsum('bnkgqs,bnskd->bnqkgd', p.astype(vb.dtype), vb)
    return out.reshape(B, S, A_HEADS * A_HEAD_DIM) @ w_o


def _rope(t, pos):
    half = ROPE_DIM // 2
    inv_freq = ROPE_THETA ** (-jnp.arange(half, dtype=jnp.float32) * 2.0 / ROPE_DIM)
    ang = pos.astype(jnp.float32)[:, :, None] * inv_freq[None, None, :]
    cos = jnp.cos(ang)[:, :, None, :]
    sin = jnp.sin(ang)[:, :, None, :]
    t32 = t.astype(jnp.float32)
    t1, t2 = t32[..., :half], t32[..., half:]
    return jnp.concatenate([t1 * cos - t2 * sin, t1 * sin + t2 * cos], axis=-1).astype(t.dtype)


def _mla(h, pos, w_in, g_q, g_kv, w_uq, w_ukv, w_o):
    B, S, _ = h.shape
    nb = S // BLK
    lat = h @ w_in
    cq, ckv, k_rope = jnp.split(lat, [Q_LORA, Q_LORA + KV_LORA], axis=-1)
    cq = _rmsnorm(cq, g_q)
    ckv = _rmsnorm(ckv, g_kv)
    q = (cq @ w_uq).reshape(B, S, B_HEADS, NOPE_DIM + ROPE_DIM)
    kv = (ckv @ w_ukv).reshape(B, S, B_HEADS, NOPE_DIM + V_DIM)
    q_nope, q_rope = q[..., :NOPE_DIM], q[..., NOPE_DIM:]
    k_nope, v = kv[..., :NOPE_DIM], kv[..., NOPE_DIM:]
    q_rope = _rope(q_rope, pos)
    k_rope = _rope(k_rope[:, :, None, :], pos)
    k = jnp.concatenate([k_nope, jnp.broadcast_to(k_rope, (B, S, B_HEADS, ROPE_DIM))], axis=-1)
    q = jnp.concatenate([q_nope, q_rope], axis=-1) * ((NOPE_DIM + ROPE_DIM) ** -0.5)
    qb = q.reshape(B, nb, BLK, B_HEADS, NOPE_DIM + ROPE_DIM).transpose(1, 0, 2, 3, 4)

    def attend(qblk):
        s = jnp.einsum('bqhd,bkhd->bhqk', qblk, k).astype(jnp.float32)
        p = jax.nn.softmax(s, axis=-1)
        return jnp.einsum('bhqk,bkhd->bqhd', p.astype(v.dtype), v)

    o = lax.map(attend, qb)
    o = o.transpose(1, 0, 2, 3, 4).reshape(B, S, B_HEADS * V_DIM)
    return o @ w_o


def _swiglu(h, w_gu, w_down):
    g, u = jnp.split(h @ w_gu, 2, axis=-1)
    return (jax.nn.silu(g) * u) @ w_down


def setup_inputs(seed: int = 0) -> dict:
    key = jax.random.key(seed)
    ks = jax.random.split(key, 20)

    def w(k, shape, fan_in):
        return jax.random.normal(k, shape, jnp.float32) * (fan_in ** -0.5)

    def gain(k, shape):
        return 1.0 + 0.02 * jax.random.normal(k, shape, jnp.float32)

    x = jax.random.normal(ks[0], (BATCH, SEQ, D_MODEL), jnp.float32)
    offset = jax.random.randint(ks[1], (BATCH, 1), 0, 1024, dtype=jnp.int32)
    positions = offset + jnp.arange(SEQ, dtype=jnp.int32)[None, :]
    return {
        "x": x,
        "positions": positions,
        "norm_mix": gain(ks[2], (DEPTH, D_MODEL)),
        "norm_ffn": gain(ks[3], (DEPTH, D_MODEL)),
        "a_w_qkv": w(ks[4], (N_A, D_MODEL, A_QKV), D_MODEL),
        "a_sink": 0.5 * jax.random.normal(ks[5], (N_A, A_HEADS), jnp.float32),
        "a_w_o": w(ks[6], (N_A, A_HEADS * A_HEAD_DIM, D_MODEL), A_HEADS * A_HEAD_DIM),
        "b_w_in": w(ks[7], (N_B, D_MODEL, B_IN), D_MODEL),
        "b_g_q": gain(ks[8], (N_B, Q_LORA)),
        "b_g_kv": gain(ks[9], (N_B, KV_LORA)),
        "b_w_uq": w(ks[10], (N_B, Q_LORA, B_HEADS * (NOPE_DIM + ROPE_DIM)), Q_LORA),
        "b_w_ukv": w(ks[11], (N_B, KV_LORA, B_HEADS * (NOPE_DIM + V_DIM)), KV_LORA),
        "b_w_o": w(ks[12], (N_B, B_HEADS * V_DIM, D_MODEL), B_HEADS * V_DIM),
        "ffn_w_gu": w(ks[13], (DEPTH, D_MODEL, 2 * D_FF), D_MODEL),
        "ffn_w_down": w(ks[14], (DEPTH, D_FF, D_MODEL), D_FF),
        "final_norm": gain(ks[15], (D_MODEL,)),
    }


def reference(x, positions, norm_mix, norm_ffn, a_w_qkv, a_sink, a_w_o, b_w_in, b_g_q, b_g_kv,
              b_w_uq, b_w_ukv, b_w_o, ffn_w_gu, ffn_w_down, final_norm):
    h = x
    for i in range(DEPTH):
        j = i // N_MIXERS
        hn = _rmsnorm(h, norm_mix[i])
        if i % N_MIXERS == 0:
            mix = _windowed_gqa(hn, positions, a_w_qkv[j], a_sink[j], a_w_o[j])
        else:
            mix = _mla(hn, positions, b_w_in[j], b_g_q[j], b_g_kv[j], b_w_uq[j], b_w_ukv[j], b_w_o[j])
        h = h + mix.astype(h.dtype)
        h = h + _swiglu(_rmsnorm(h, norm_ffn[i]), ffn_w_gu[i], ffn_w_down[i]).astype(h.dtype)
    return _rmsnorm(h, final_norm)
```

```python
import functools
import math

import numpy as np
import jax
import jax.numpy as jnp
from jax import lax
from jax.experimental import pallas as pl
from jax.experimental.pallas import tpu as pltpu

F32 = jnp.float32
BF16 = jnp.bfloat16

D_MODEL = 1024
DEPTH = 4
A_HEADS = 16
A_KV_HEADS = 4
A_GROUP = A_HEADS // A_KV_HEADS
A_HEAD_DIM = 64
WINDOW = 128
BLK = 128
B_HEADS = 16
Q_LORA = 384
KV_LORA = 256
NOPE_DIM = 64
ROPE_DIM = 32
V_DIM = 64
ROPE_THETA = 10000.0
D_FF = 2816
EPS = 1e-6
NEG = -1e30

LANES = 128
VMEM_LIMIT = 56 * 1024 * 1024

TM_PROJ = 512
TQ_WIN = 256
TM_FFN = 1024
TF_FFN = 256
TM_PREP = 256
TQ_MLA = 512

NT_DIMS = (((1,), (1,)), ((), ()))


def _rms(x, g):
    ms = jnp.mean(x * x, axis=-1, keepdims=True)
    return x * lax.rsqrt(ms + EPS) * g


def _params(n_axes):
    return pltpu.CompilerParams(dimension_semantics=("arbitrary",) * n_axes,
                                vmem_limit_bytes=VMEM_LIMIT)


def _norm_qkv_kernel(x_ref, g_ref, w_ref, q_ref, k_ref, v_ref):
    xn = _rms(x_ref[...], g_ref[...]).astype(BF16)
    qkv = jnp.dot(xn, w_ref[...], preferred_element_type=F32)
    nq = A_HEADS * A_HEAD_DIM
    nk = A_KV_HEADS * A_HEAD_DIM
    q_ref[...] = qkv[:, :nq].astype(BF16)
    k_ref[...] = qkv[:, nq:nq + nk].astype(BF16)
    v_ref[...] = qkv[:, nq + nk:].astype(BF16)


def _norm_qkv(x, g, w):
    T, D = x.shape
    N = w.shape[1]
    nq = A_HEADS * A_HEAD_DIM
    nk = A_KV_HEADS * A_HEAD_DIM
    tm = TM_PROJ
    return pl.pallas_call(
        _norm_qkv_kernel,
        grid=(T // tm,),
        in_specs=[pl.BlockSpec((tm, D), lambda i: (i, 0)),
                  pl.BlockSpec((1, D), lambda i: (0, 0)),
                  pl.BlockSpec((D, N), lambda i: (0, 0))],
        out_specs=[pl.BlockSpec((tm, nq), lambda i: (i, 0)),
                   pl.BlockSpec((tm, nk), lambda i: (i, 0)),
                   pl.BlockSpec((tm, nk), lambda i: (i, 0))],
        out_shape=[jax.ShapeDtypeStruct((T, nq), BF16),
                   jax.ShapeDtypeStruct((T, nk), BF16),
                   jax.ShapeDtypeStruct((T, nk), BF16)],
        compiler_params=_params(1),
        name="norm_qkv",
    )(x, g, w)


def _alibi_slope(h):
    return float(2.0 ** (-8.0 * (h + 1) / A_HEADS))


def _win_attn_kernel(sink_ref, q_ref, kp_ref, km_ref, kn_ref, vp_ref, vm_ref, vn_ref,
                     pc_ref, prp_ref, prm_ref, prn_ref, o_ref, *, n_blocks):
    j = pl.program_id(1)
    sub = TQ_WIN // BLK
    nkey = 3 * BLK
    kwin = jnp.concatenate([kp_ref[...], km_ref[...], kn_ref[...]], axis=0)
    vwin = jnp.concatenate([vp_ref[...], vm_ref[...], vn_ref[...]], axis=0)
    prow = jnp.concatenate([prp_ref[...], prm_ref[...], prn_ref[...]], axis=1)
    lane = lax.broadcasted_iota(jnp.int32, (1, LANES), 1)
    low = lane < A_HEAD_DIM
    r = lax.broadcasted_iota(jnp.int32, (BLK, nkey), 0)
    c = lax.broadcasted_iota(jnp.int32, (BLK, nkey), 1)
    d = c - r
    band = (d >= BLK - WINDOW) & (d <= BLK + WINDOW)
    for jj in range(sub):
        blk = j * sub + jj
        valid = band & ((c >= BLK) | (blk > 0)) & ((c < 2 * BLK) | (blk < n_blocks - 1))
        qpos = pc_ref[jj * BLK:(jj + 1) * BLK, :]
        kpos = prow[:, jj * BLK:jj * BLK + nkey]
        dist = jnp.abs(qpos - kpos).astype(F32)
        kk = kwin[jj * BLK:jj * BLK + nkey, :].astype(F32)
        vv = vwin[jj * BLK:jj * BLK + nkey, :].astype(F32)
        for kh in range(A_KV_HEADS):
            pair = kh // 2
            kpair = kk[:, pair * LANES:(pair + 1) * LANES]
            vpair = vv[:, pair * LANES:(pair + 1) * LANES]
            kswap = pltpu.roll(kpair, A_HEAD_DIM, 1)
            vswap = pltpu.roll(vpair, A_HEAD_DIM, 1)
            if kh % 2 == 0:
                k_lo = jnp.where(low, kpair, 0.0)
                k_hi = jnp.where(low, 0.0, kswap)
                vdup = jnp.where(low, vpair, vswap)
            else:
                k_lo = jnp.where(low, kswap, 0.0)
                k_hi = jnp.where(low, 0.0, kpair)
                vdup = jnp.where(low, vswap, vpair)
            kcat = jnp.concatenate([k_lo, k_hi], axis=0).astype(BF16)
            vdup = vdup.astype(BF16)
            for half in range(A_GROUP // 2):
                col = kh * A_GROUP * A_HEAD_DIM + half * LANES
                qq = q_ref[jj * BLK:(jj + 1) * BLK, col:col + LANES]
                qq = (qq.astype(F32) * (A_HEAD_DIM ** -0.5)).astype(BF16)
                s2 = lax.dot_general(qq, kcat, NT_DIMS, preferred_element_type=F32)
                outs = []
                for e in range(2):
                    h = kh * A_GROUP + 2 * half + e
                    s = s2[:, e * nkey:(e + 1) * nkey] - _alibi_slope(h) * dist
                    s = jnp.where(valid, s, NEG)
                    sink = sink_ref[h]
                    m = jnp.maximum(jnp.max(s, axis=-1, keepdims=True), sink)
                    p = jnp.exp(s - m)
                    l = jnp.sum(p, axis=-1, keepdims=True) + jnp.exp(sink - m)
                    o = jnp.dot(p.astype(BF16), vdup, preferred_element_type=F32)
                    outs.append(o / l)
                o_ref[jj * BLK:(jj + 1) * BLK, col:col + LANES] = (
                    jnp.where(low, outs[0], outs[1]).astype(BF16))


def _win_attn(q, k, v, pos_col, pos_row, sink):
    B, S, nq = q.shape
    nk = k.shape[2]
    nb = S // BLK
    sub = TQ_WIN // BLK
    prev_idx = lambda b, j: (b, jnp.maximum(j * sub - 1, 0), 0)
    main_idx = lambda b, j: (b, j, 0)
    next_idx = lambda b, j: (b, jnp.minimum((j + 1) * sub, nb - 1), 0)
    prev_row = lambda b, j: (b, 0, jnp.maximum(j * sub - 1, 0))
    main_row = lambda b, j: (b, 0, j)
    next_row = lambda b, j: (b, 0, jnp.minimum((j + 1) * sub, nb - 1))
    return pl.pallas_call(
        functools.partial(_win_attn_kernel, n_blocks=nb),
        grid=(B, S // TQ_WIN),
        in_specs=[pl.BlockSpec(memory_space=pltpu.SMEM),
                  pl.BlockSpec((None, TQ_WIN, nq), main_idx),
                  pl.BlockSpec((None, BLK, nk), prev_idx),
                  pl.BlockSpec((None, TQ_WIN, nk), main_idx),
                  pl.BlockSpec((None, BLK, nk), next_idx),
                  pl.BlockSpec((None, BLK, nk), prev_idx),
                  pl.BlockSpec((None, TQ_WIN, nk), main_idx),
                  pl.BlockSpec((None, BLK, nk), next_idx),
                  pl.BlockSpec((None, TQ_WIN, 1), main_idx),
                  pl.BlockSpec((None, 1, BLK), prev_row),
                  pl.BlockSpec((None, 1, TQ_WIN), main_row),
                  pl.BlockSpec((None, 1, BLK), next_row)],
        out_specs=pl.BlockSpec((None, TQ_WIN, nq), main_idx),
        out_shape=jax.ShapeDtypeStruct((B, S, nq), BF16),
        compiler_params=_params(2),
        name="win_attn",
    )(sink, q, k, k, k, v, v, v, pos_col, pos_row, pos_row, pos_row)


def _proj_ffn_kernel(h_ref, a_ref, wo_ref, g_ref, wg_ref, wu_ref, wd_ref, gf_ref, out_ref,
                     xn_sc, acc_sc, *, n_f, final_norm):
    f = pl.program_id(1)

    @pl.when(f == 0)
    def _():
        h1 = h_ref[...] + jnp.dot(a_ref[...], wo_ref[...], preferred_element_type=F32)
        acc_sc[...] = h1
        xn_sc[...] = _rms(h1, g_ref[...]).astype(BF16)

    xn = xn_sc[...]
    gate = jnp.dot(xn, wg_ref[...], preferred_element_type=F32)
    up = jnp.dot(xn, wu_ref[...], preferred_element_type=F32)
    act = (gate * jax.nn.sigmoid(gate) * up).astype(BF16)
    acc_sc[...] += jnp.dot(act, wd_ref[...], preferred_element_type=F32)

    @pl.when(f == n_f - 1)
    def _():
        y = acc_sc[...]
        if final_norm:
            y = _rms(y, gf_ref[...])
        out_ref[...] = y


def _proj_ffn(h, a, wo, g, wgu, wd, gf, final_norm):
    T, D = h.shape
    tm, tf = TM_FFN, TF_FFN
    n_f = D_FF // tf
    return pl.pallas_call(
        functools.partial(_proj_ffn_kernel, n_f=n_f, final_norm=final_norm),
        grid=(T // tm, n_f),
        in_specs=[pl.BlockSpec((tm, D), lambda i, f: (i, 0)),
                  pl.BlockSpec((tm, a.shape[1]), lambda i, f: (i, 0)),
                  pl.BlockSpec(wo.shape, lambda i, f: (0, 0)),
                  pl.BlockSpec((1, D), lambda i, f: (0, 0)),
                  pl.BlockSpec((D, tf), lambda i, f: (0, f)),
                  pl.BlockSpec((D, tf), lambda i, f: (0, f + n_f)),
                  pl.BlockSpec((tf, D), lambda i, f: (f, 0)),
                  pl.BlockSpec((1, D), lambda i, f: (0, 0))],
        out_specs=pl.BlockSpec((tm, D), lambda i, f: (i, 0)),
        out_shape=jax.ShapeDtypeStruct((T, D), F32),
        scratch_shapes=[pltpu.VMEM((tm, D), BF16), pltpu.VMEM((tm, D), F32)],
        compiler_params=_params(2),
        name="proj_ffn",
    )(h, a, wo, g, wgu, wgu, wd, gf)


def _rope_table_kernel(pos_ref, inv_ref, ct_ref, s1_ref, s2_ref):
    ang = pos_ref[...].astype(F32) * inv_ref[...]
    lane = lax.broadcasted_iota(jnp.int32, ang.shape, 1)
    half = ROPE_DIM // 2
    first = (lane >= NOPE_DIM) & (lane < NOPE_DIM + half)
    second = (lane >= NOPE_DIM + half) & (lane < NOPE_DIM + ROPE_DIM)
    cos = jnp.cos(ang)
    sin = jnp.sin(ang)
    ct_ref[...] = jnp.where(first | second, cos, 1.0)
    s1_ref[...] = jnp.where(first, -sin, 0.0)
    s2_ref[...] = jnp.where(second, sin, 0.0)


def _rope_tables(pos_col, inv_lane):
    T = pos_col.shape[0]
    tm = 2048
    tab = jax.ShapeDtypeStruct((T, LANES), F32)
    return pl.pallas_call(
        _rope_table_kernel,
        grid=(T // tm,),
        in_specs=[pl.BlockSpec((tm, 1), lambda i: (i, 0)),
                  pl.BlockSpec((1, LANES), lambda i: (0, 0))],
        out_specs=[pl.BlockSpec((tm, LANES), lambda i: (i, 0))] * 3,
        out_shape=[tab, tab, tab],
        compiler_params=_params(1),
        name="rope_tables",
    )(pos_col, inv_lane)


def _mla_prep_kernel(x_ref, g_ref, win_ref, gq_ref, gkv_ref, wuq_ref, wuk_ref, wuv_ref,
                     ct_ref, s1_ref, s2_ref, q_ref, k_ref, v_ref):
    xn = _rms(x_ref[...], g_ref[...]).astype(BF16)
    lat = jnp.dot(xn, win_ref[...], preferred_element_type=F32)
    cq = _rms(lat[:, :Q_LORA], gq_ref[...]).astype(BF16)
    ckv = _rms(lat[:, Q_LORA:Q_LORA + KV_LORA], gkv_ref[...]).astype(BF16)
    k_rope = lat[:, Q_LORA + KV_LORA:]
    ct = ct_ref[...]
    s1 = s1_ref[...]
    s2 = s2_ref[...]
    half = ROPE_DIM // 2

    def rope(t):
        return t * ct + pltpu.roll(t, LANES - half, 1) * s1 + pltpu.roll(t, half, 1) * s2

    k_rope = rope(k_rope)
    scale = (NOPE_DIM + ROPE_DIM) ** -0.5
    q = jnp.dot(cq, wuq_ref[...], preferred_element_type=F32)
    kn = jnp.dot(ckv, wuk_ref[...], preferred_element_type=F32)
    v_ref[...] = jnp.dot(ckv, wuv_ref[...], preferred_element_type=F32).astype(BF16)
    for h in range(B_HEADS):
        sl = slice(h * LANES, (h + 1) * LANES)
        q_ref[h] = (rope(q[:, sl]) * scale).astype(BF16)
        k_ref[h] = (kn[:, sl] + k_rope).astype(BF16)


def _mla_prep(x, g, win, gq, gkv, wuq, wuk, wuv, ct, s1, s2):
    B, S, D = x.shape
    tm = TM_PREP
    row = lambda b, i: (b, i, 0)
    const = lambda b, i: (0, 0)
    head = lambda b, i: (b, 0, i, 0)
    return pl.pallas_call(
        _mla_prep_kernel,
        grid=(B, S // tm),
        in_specs=[pl.BlockSpec((None, tm, D), row),
                  pl.BlockSpec(g.shape, const),
                  pl.BlockSpec(win.shape, const),
                  pl.BlockSpec(gq.shape, const),
                  pl.BlockSpec(gkv.shape, const),
                  pl.BlockSpec(wuq.shape, const),
                  pl.BlockSpec(wuk.shape, const),
                  pl.BlockSpec(wuv.shape, const),
                  pl.BlockSpec((None, tm, LANES), row),
                  pl.BlockSpec((None, tm, LANES), row),
                  pl.BlockSpec((None, tm, LANES), row)],
        out_specs=[pl.BlockSpec((None, B_HEADS, tm, LANES), head),
                   pl.BlockSpec((None, B_HEADS, tm, LANES), head),
                   pl.BlockSpec((None, tm, B_HEADS * V_DIM), row)],
        out_shape=[jax.ShapeDtypeStruct((B, B_HEADS, S, LANES), BF16),
                   jax.ShapeDtypeStruct((B, B_HEADS, S, LANES), BF16),
                   jax.ShapeDtypeStruct((B, S, B_HEADS * V_DIM), BF16)],
        compiler_params=_params(2),
        name="mla_prep",
    )(x, g, win, gq, gkv, wuq, wuk, wuv, ct, s1, s2)


def _mla_attn_kernel(q_ref, k_ref, v_ref, o_ref):
    lane = lax.broadcasted_iota(jnp.int32, (1, LANES), 1)
    vpair = v_ref[...]
    outs = []
    for e in range(2):
        s = lax.dot_general(q_ref[e], k_ref[e], NT_DIMS, preferred_element_type=F32)
        m = jnp.max(s, axis=-1, keepdims=True)
        p = jnp.exp(s - m)
        l = jnp.sum(p, axis=-1, keepdims=True)
        o = jnp.dot(p.astype(BF16), vpair, preferred_element_type=F32)
        outs.append(o / l)
    o_ref[...] = jnp.where(lane < V_DIM, outs[0], outs[1]).astype(BF16)


def _mla_attn(q, k, v):
    B, H, S, _ = q.shape
    tq = TQ_MLA
    return pl.pallas_call(
        _mla_attn_kernel,
        grid=(B, H // 2, S // tq),
        in_specs=[pl.BlockSpec((None, 2, tq, LANES), lambda b, hp, i: (b, hp, i, 0)),
                  pl.BlockSpec((None, 2, S, LANES), lambda b, hp, i: (b, hp, 0, 0)),
                  pl.BlockSpec((None, S, LANES), lambda b, hp, i: (b, 0, hp))],
        out_specs=pl.BlockSpec((None, tq, LANES), lambda b, hp, i: (b, i, hp)),
        out_shape=jax.ShapeDtypeStruct((B, S, H * V_DIM), BF16),
        compiler_params=_params(3),
        name="mla_attn",
    )(q, k, v)


def _pad_heads(w, width):
    kdim = w.shape[0]
    w = w.reshape(kdim, B_HEADS, width)
    w = jnp.pad(w, ((0, 0), (0, 0), (0, LANES - width)))
    return w.reshape(kdim, B_HEADS * LANES)


def kernel(x, positions, norm_mix, norm_ffn, a_w_qkv, a_sink, a_w_o, b_w_in, b_g_q, b_g_kv,
           b_w_uq, b_w_ukv, b_w_o, ffn_w_gu, ffn_w_down, final_norm):
    B, S, D = x.shape
    T = B * S
    h = x.reshape(T, D)
    pos_col3 = positions.reshape(B, S, 1)
    pos_row3 = positions.reshape(B, 1, S)

    half = ROPE_DIM // 2
    inv_freq = ROPE_THETA ** (-jnp.arange(half, dtype=F32) * 2.0 / ROPE_DIM)
    inv_lane = jnp.zeros((1, LANES), F32)
    inv_lane = inv_lane.at[0, NOPE_DIM:NOPE_DIM + half].set(inv_freq)
    inv_lane = inv_lane.at[0, NOPE_DIM + half:NOPE_DIM + ROPE_DIM].set(inv_freq)
    ct, s1, s2 = _rope_tables(positions.reshape(T, 1), inv_lane)
    ct, s1, s2 = (t.reshape(B, S, LANES) for t in (ct, s1, s2))

    gf = final_norm.reshape(1, D)
    for i in range(DEPTH):
        j = i // 2
        g_mix = norm_mix[i].reshape(1, D)
        if i % 2 == 0:
            q, k, v = _norm_qkv(h, g_mix, a_w_qkv[j].astype(BF16))
            nq = q.shape[1]
            nk = k.shape[1]
            attn = _win_attn(q.reshape(B, S, nq), k.reshape(B, S, nk), v.reshape(B, S, nk),
                             pos_col3, pos_row3, a_sink[j])
            attn = attn.reshape(T, nq)
            w_o = a_w_o[j].astype(BF16)
        else:
            w_in = b_w_in[j]
            pad_k = jnp.zeros((D, NOPE_DIM), F32)
            pad_t = jnp.zeros((D, LANES - NOPE_DIM - ROPE_DIM), F32)
            w_in = jnp.concatenate([w_in[:, :Q_LORA + KV_LORA], pad_k,
                                    w_in[:, Q_LORA + KV_LORA:], pad_t], axis=1).astype(BF16)
            w_uq = _pad_heads(b_w_uq[j], NOPE_DIM + ROPE_DIM).astype(BF16)
            w_ukv = b_w_ukv[j].reshape(KV_LORA, B_HEADS, NOPE_DIM + V_DIM)
            w_uk = _pad_heads(w_ukv[:, :, :NOPE_DIM].reshape(KV_LORA, B_HEADS * NOPE_DIM),
                              NOPE_DIM).astype(BF16)
            w_uv = w_ukv[:, :, NOPE_DIM:].reshape(KV_LORA, B_HEADS * V_DIM).astype(BF16)
            qh, kh, v = _mla_prep(h.reshape(B, S, D), g_mix, w_in,
                                  b_g_q[j].reshape(1, Q_LORA), b_g_kv[j].reshape(1, KV_LORA),
                                  w_uq, w_uk, w_uv, ct, s1, s2)
            attn = _mla_attn(qh, kh, v).reshape(T, B_HEADS * V_DIM)
            w_o = b_w_o[j].astype(BF16)
        h = _proj_ffn(h, attn, w_o, norm_ffn[i].reshape(1, D),
                      ffn_w_gu[i].astype(BF16), ffn_w_down[i].astype(BF16),
                      gf, final_norm=(i == DEPTH - 1))
    return h.reshape(B, S, D)
```

```python
import functools
import math

import jax
import jax.numpy as jnp
from jax import lax
from jax.experimental import pallas as pl
from jax.experimental.pallas import tpu as pltpu

F32 = jnp.float32
BF16 = jnp.bfloat16

D_MODEL = 1024
DEPTH = 4
A_HEADS = 16
A_KV_HEADS = 4
A_GROUP = A_HEADS // A_KV_HEADS
A_HEAD_DIM = 64
WINDOW = 128
BLK = 128
B_HEADS = 16
Q_LORA = 384
KV_LORA = 256
NOPE_DIM = 64
ROPE_DIM = 32
V_DIM = 64
ROPE_THETA = 10000.0
D_FF = 2816
EPS = 1e-6

LOG2E = math.log2(math.e)
MASK_DIST = 1e36

LANES = 128
VMEM_LIMIT = 56 * 1024 * 1024

TM_PROJ = 512
TM_FFN = 1024
TF_FFN = 256
TM_PREP = 256
TQ_MLA = 256

NT_DIMS = (((1,), (1,)), ((), ()))


def _rms(x, g):
    ms = jnp.mean(x * x, axis=-1, keepdims=True)
    return x * lax.rsqrt(ms + EPS) * g


def _params(n_axes):
    return pltpu.CompilerParams(dimension_semantics=("arbitrary",) * n_axes,
                                vmem_limit_bytes=VMEM_LIMIT)


def _norm_qkv_kernel(x_ref, g_ref, w_ref, q_ref, k_ref, v_ref):
    xn = _rms(x_ref[...], g_ref[...]).astype(BF16)
    qkv = jnp.dot(xn, w_ref[...], preferred_element_type=F32)
    nq = A_HEADS * A_HEAD_DIM
    nk = A_KV_HEADS * A_HEAD_DIM
    q_ref[...] = (qkv[:, :nq] * (A_HEAD_DIM ** -0.5 * LOG2E)).astype(BF16)
    k_ref[...] = qkv[:, nq:nq + nk].astype(BF16)
    v_ref[...] = qkv[:, nq + nk:].astype(BF16)


def _norm_qkv(x, g, w):
    T, D = x.shape
    N = w.shape[1]
    nq = A_HEADS * A_HEAD_DIM
    nk = A_KV_HEADS * A_HEAD_DIM
    tm = TM_PROJ
    return pl.pallas_call(
        _norm_qkv_kernel,
        grid=(T // tm,),
        in_specs=[pl.BlockSpec((tm, D), lambda i: (i, 0)),
                  pl.BlockSpec((1, D), lambda i: (0, 0)),
                  pl.BlockSpec((D, N), lambda i: (0, 0))],
        out_specs=[pl.BlockSpec((tm, nq), lambda i: (i, 0)),
                   pl.BlockSpec((tm, nk), lambda i: (i, 0)),
                   pl.BlockSpec((tm, nk), lambda i: (i, 0))],
        out_shape=[jax.ShapeDtypeStruct((T, nq), BF16),
                   jax.ShapeDtypeStruct((T, nk), BF16),
                   jax.ShapeDtypeStruct((T, nk), BF16)],
        compiler_params=_params(1),
        name="norm_qkv",
    )(x, g, w)


def _alibi_slope(h):
    return float(2.0 ** (-8.0 * (h + 1) / A_HEADS))


def _win_attn_kernel(sink_ref, q_ref, k_ref, v_ref, pc_ref, pr_ref, o_ref, *, n_blocks):
    nkey = 3 * BLK
    lane = lax.broadcasted_iota(jnp.int32, (1, LANES), 1)
    low = lane < A_HEAD_DIM
    r = lax.broadcasted_iota(jnp.int32, (BLK, nkey), 0)
    c = lax.broadcasted_iota(jnp.int32, (BLK, nkey), 1)
    cr = c - r

    def body(blk, carry):
        wb = jnp.clip(blk - 1, 0, n_blocks - 3)
        r0 = pl.multiple_of(blk * BLK, BLK)
        w0 = pl.multiple_of(wb * BLK, BLK)
        valid = jnp.abs(cr + (wb - blk) * BLK) <= WINDOW
        qpos = pc_ref[pl.ds(r0, BLK), :]
        kpos = jnp.concatenate([pr_ref[pl.ds(wb + t, 1), :] for t in range(3)], axis=1)
        dist = jnp.abs(qpos - kpos).astype(F32)
        dm = jnp.where(valid, dist, MASK_DIST)
        kk = k_ref[pl.ds(w0, nkey), :].astype(F32)
        vv = v_ref[pl.ds(w0, nkey), :].astype(F32)
        for kh in range(A_KV_HEADS):
            pair = kh // 2
            kpair = kk[:, pair * LANES:(pair + 1) * LANES]
            vpair = vv[:, pair * LANES:(pair + 1) * LANES]
            kswap = pltpu.roll(kpair, A_HEAD_DIM, 1)
            vswap = pltpu.roll(vpair, A_HEAD_DIM, 1)
            k_own, k_other = (kpair, kswap) if kh % 2 == 0 else (kswap, kpair)
            v_own, v_other = (vpair, vswap) if kh % 2 == 0 else (vswap, vpair)
            k_lo = jnp.where(low, k_own, 0.0)
            k_hi = jnp.where(low, 0.0, k_other)
            kcat = jnp.concatenate([k_lo, k_hi], axis=0).astype(BF16)
            v_lo = jnp.where(low, v_own, 1.0).astype(BF16)
            v_hi = jnp.where(low, 1.0, v_other).astype(BF16)
            col = kh * A_GROUP * A_HEAD_DIM
            qst = jnp.concatenate([q_ref[pl.ds(r0, BLK), col:col + LANES],
                                   q_ref[pl.ds(r0, BLK), col + LANES:col + 2 * LANES]], axis=0)
            s2 = lax.dot_general(qst, kcat, NT_DIMS, preferred_element_type=F32)
            probs, sink_terms = [], []
            for g in range(A_GROUP):
                half, e = divmod(g, 2)
                h = kh * A_GROUP + g
                s = (s2[half * BLK:(half + 1) * BLK, e * nkey:(e + 1) * nkey]
                     - (_alibi_slope(h) * LOG2E) * dm)
                sink = sink_ref[h] * LOG2E
                m = jnp.maximum(jnp.max(s, axis=-1, keepdims=True), sink)
                probs.append(jnp.exp2(s - m).astype(BF16))
                sink_terms.append(jnp.exp2(sink - m))
            o_even = jnp.dot(jnp.concatenate([probs[0], probs[2]], axis=0), v_lo,
                             preferred_element_type=F32)
            o_odd = jnp.dot(jnp.concatenate([probs[1], probs[3]], axis=0), v_hi,
                            preferred_element_type=F32)
            for half in range(A_GROUP // 2):
                oe = o_even[half * BLK:(half + 1) * BLK]
                oo = o_odd[half * BLK:(half + 1) * BLK]
                oe = oe / (pltpu.roll(oe, A_HEAD_DIM, 1) + sink_terms[2 * half])
                oo = oo / (pltpu.roll(oo, A_HEAD_DIM, 1) + sink_terms[2 * half + 1])
                o_ref[pl.ds(r0, BLK), col + half * LANES:col + (half + 1) * LANES] = (
                    jnp.where(low, oe, oo).astype(BF16))
        return carry

    lax.fori_loop(0, n_blocks, body, 0)


def _win_attn(q, k, v, pos_col, pos_blk, sink):
    B, S, nq = q.shape
    nk = k.shape[2]
    nb = S // BLK
    whole = lambda b: (b, 0, 0)
    return pl.pallas_call(
        functools.partial(_win_attn_kernel, n_blocks=nb),
        grid=(B,),
        in_specs=[pl.BlockSpec(memory_space=pltpu.SMEM),
                  pl.BlockSpec((None, S, nq), whole),
                  pl.BlockSpec((None, S, nk), whole),
                  pl.BlockSpec((None, S, nk), whole),
                  pl.BlockSpec((None, S, 1), whole),
                  pl.BlockSpec((None, nb, BLK), whole)],
        out_specs=pl.BlockSpec((None, S, nq), whole),
        out_shape=jax.ShapeDtypeStruct((B, S, nq), BF16),
        compiler_params=_params(1),
        name="win_attn",
    )(sink, q, k, v, pos_col, pos_blk)


def _proj_ffn_kernel(h_ref, a_ref, wo_ref, g_ref, wg_ref, wu_ref, wd_ref, gf_ref, out_ref,
                     xn_sc, acc_sc, *, n_f, final_norm):
    f = pl.program_id(1)

    @pl.when(f == 0)
    def _():
        h1 = h_ref[...] + jnp.dot(a_ref[...], wo_ref[...], preferred_element_type=F32)
        acc_sc[...] = h1
        xn_sc[...] = _rms(h1, g_ref[...]).astype(BF16)

    xn = xn_sc[...]
    gate = jnp.dot(xn, wg_ref[...], preferred_element_type=F32)
    up = jnp.dot(xn, wu_ref[...], preferred_element_type=F32)
    act = (gate * jax.nn.sigmoid(gate) * up).astype(BF16)
    acc_sc[...] += jnp.dot(act, wd_ref[...], preferred_element_type=F32)

    @pl.when(f == n_f - 1)
    def _():
        y = acc_sc[...]
        if final_norm:
            y = _rms(y, gf_ref[...])
        out_ref[...] = y


def _proj_ffn(h, a, wo, g, wgu, wd, gf, final_norm):
    T, D = h.shape
    tm, tf = TM_FFN, TF_FFN
    n_f = D_FF // tf
    return pl.pallas_call(
        functools.partial(_proj_ffn_kernel, n_f=n_f, final_norm=final_norm),
        grid=(T // tm, n_f),
        in_specs=[pl.BlockSpec((tm, D), lambda i, f: (i, 0)),
                  pl.BlockSpec((tm, a.shape[1]), lambda i, f: (i, 0)),
                  pl.BlockSpec(wo.shape, lambda i, f: (0, 0)),
                  pl.BlockSpec((1, D), lambda i, f: (0, 0)),
                  pl.BlockSpec((D, tf), lambda i, f: (0, f)),
                  pl.BlockSpec((D, tf), lambda i, f: (0, f + n_f)),
                  pl.BlockSpec((tf, D), lambda i, f: (f, 0)),
                  pl.BlockSpec((1, D), lambda i, f: (0, 0))],
        out_specs=pl.BlockSpec((tm, D), lambda i, f: (i, 0)),
        out_shape=jax.ShapeDtypeStruct((T, D), F32),
        scratch_shapes=[pltpu.VMEM((tm, D), BF16), pltpu.VMEM((tm, D), F32)],
        compiler_params=_params(2),
        name="proj_ffn",
    )(h, a, wo, g, wgu, wgu, wd, gf)


def _rope_table_kernel(pos_ref, inv_ref, ct_ref, s1_ref, s2_ref):
    ang = pos_ref[...].astype(F32) * inv_ref[...]
    lane = lax.broadcasted_iota(jnp.int32, ang.shape, 1)
    half = ROPE_DIM // 2
    first = (lane >= NOPE_DIM) & (lane < NOPE_DIM + half)
    second = (lane >= NOPE_DIM + half) & (lane < NOPE_DIM + ROPE_DIM)
    cos = jnp.cos(ang)
    sin = jnp.sin(ang)
    ct_ref[...] = jnp.where(first | second, cos, 1.0)
    s1_ref[...] = jnp.where(first, -sin, 0.0)
    s2_ref[...] = jnp.where(second, sin, 0.0)


def _rope_tables(pos_col, inv_lane):
    T = pos_col.shape[0]
    tm = 2048
    tab = jax.ShapeDtypeStruct((T, LANES), F32)
    return pl.pallas_call(
        _rope_table_kernel,
        grid=(T // tm,),
        in_specs=[pl.BlockSpec((tm, 1), lambda i: (i, 0)),
                  pl.BlockSpec((1, LANES), lambda i: (0, 0))],
        out_specs=[pl.BlockSpec((tm, LANES), lambda i: (i, 0))] * 3,
        out_shape=[tab, tab, tab],
        compiler_params=_params(1),
        name="rope_tables",
    )(pos_col, inv_lane)


def _mla_prep_kernel(x_ref, g_ref, win_ref, gq_ref, gkv_ref, wuq_ref, wuk_ref, wuv_ref,
                     ct_ref, s1_ref, s2_ref, q_ref, k_ref, v_ref):
    xn = _rms(x_ref[...], g_ref[...]).astype(BF16)
    lat = jnp.dot(xn, win_ref[...], preferred_element_type=F32)
    cq = _rms(lat[:, :Q_LORA], gq_ref[...]).astype(BF16)
    ckv = _rms(lat[:, Q_LORA:Q_LORA + KV_LORA], gkv_ref[...]).astype(BF16)
    k_rope = lat[:, Q_LORA + KV_LORA:]
    ct = ct_ref[...]
    s1 = s1_ref[...]
    s2 = s2_ref[...]
    half = ROPE_DIM // 2

    def rope(t):
        return t * ct + pltpu.roll(t, LANES - half, 1) * s1 + pltpu.roll(t, half, 1) * s2

    k_rope = rope(k_rope)
    scale = (NOPE_DIM + ROPE_DIM) ** -0.5 * LOG2E
    q = jnp.dot(cq, wuq_ref[...], preferred_element_type=F32)
    kn = jnp.dot(ckv, wuk_ref[...], preferred_element_type=F32)
    v = jnp.dot(ckv, wuv_ref[...], preferred_element_type=F32)
    low = lax.broadcasted_iota(jnp.int32, (1, LANES), 1) < V_DIM
    for h in range(B_HEADS):
        sl = slice(h * LANES, (h + 1) * LANES)
        q_ref[h] = (rope(q[:, sl]) * scale).astype(BF16)
        k_ref[h] = (kn[:, sl] + k_rope).astype(BF16)
        vpair = v[:, (h // 2) * LANES:(h // 2 + 1) * LANES]
        v_aug = jnp.where(low, vpair, 1.0) if h % 2 == 0 else jnp.where(low, 1.0, vpair)
        v_ref[h] = v_aug.astype(BF16)


def _mla_prep(x, g, win, gq, gkv, wuq, wuk, wuv, ct, s1, s2):
    B, S, D = x.shape
    tm = TM_PREP
    row = lambda b, i: (b, i, 0)
    const = lambda b, i: (0, 0)
    head = lambda b, i: (b, 0, i, 0)
    return pl.pallas_call(
        _mla_prep_kernel,
        grid=(B, S // tm),
        in_specs=[pl.BlockSpec((None, tm, D), row),
                  pl.BlockSpec(g.shape, const),
                  pl.BlockSpec(win.shape, const),
                  pl.BlockSpec(gq.shape, const),
                  pl.BlockSpec(gkv.shape, const),
                  pl.BlockSpec(wuq.shape, const),
                  pl.BlockSpec(wuk.shape, const),
                  pl.BlockSpec(wuv.shape, const),
                  pl.BlockSpec((None, tm, LANES), row),
                  pl.BlockSpec((None, tm, LANES), row),
                  pl.BlockSpec((None, tm, LANES), row)],
        out_specs=[pl.BlockSpec((None, B_HEADS, tm, LANES), head),
                   pl.BlockSpec((None, B_HEADS, tm, LANES), head),
                   pl.BlockSpec((None, B_HEADS, tm, LANES), head)],
        out_shape=[jax.ShapeDtypeStruct((B, B_HEADS, S, LANES), BF16)] * 3,
        compiler_params=_params(2),
        name="mla_prep",
    )(x, g, win, gq, gkv, wuq, wuk, wuv, ct, s1, s2)


def _mla_attn_kernel(q_ref, k_ref, v_ref, o_ref, s0_sc, m0_sc, s1_sc, m1_sc):
    S = q_ref.shape[1]
    n = S // TQ_MLA
    low = lax.broadcasted_iota(jnp.int32, (1, LANES), 1) < V_DIM

    def scores(blk, s_sc, m_sc):
        r0 = pl.multiple_of(blk * TQ_MLA, TQ_MLA)
        for e in range(2):
            q = q_ref[e, pl.ds(r0, TQ_MLA), :]
            s = lax.dot_general(q, k_ref[e], NT_DIMS, preferred_element_type=F32)
            s_sc[e] = s
            m_sc[e] = jnp.max(s, axis=-1, keepdims=True)

    def context(blk, s_sc, m_sc):
        r0 = pl.multiple_of(blk * TQ_MLA, TQ_MLA)
        outs = []
        for e in range(2):
            p = jnp.exp2(s_sc[e] - m_sc[e]).astype(BF16)
            o = jnp.dot(p, v_ref[e], preferred_element_type=F32)
            outs.append(o / pltpu.roll(o, V_DIM, 1))
        o_ref[pl.ds(r0, TQ_MLA), :] = jnp.where(low, outs[0], outs[1]).astype(BF16)

    scores(0, s0_sc, m0_sc)

    def body(j, carry):
        context(2 * j, s0_sc, m0_sc)
        scores(2 * j + 1, s1_sc, m1_sc)
        context(2 * j + 1, s1_sc, m1_sc)
        scores(2 * j + 2, s0_sc, m0_sc)
        return carry

    lax.fori_loop(0, n // 2 - 1, body, 0)
    context(n - 2, s0_sc, m0_sc)
    scores(n - 1, s1_sc, m1_sc)
    context(n - 1, s1_sc, m1_sc)


def _mla_attn(q, k, v):
    B, H, S, _ = q.shape
    pair = lambda b, hp: (b, hp, 0, 0)
    s_buf = pltpu.VMEM((2, TQ_MLA, S), F32)
    m_buf = pltpu.VMEM((2, TQ_MLA, 1), F32)
    return pl.pallas_call(
        _mla_attn_kernel,
        grid=(B, H // 2),
        in_specs=[pl.BlockSpec((None, 2, S, LANES), pair)] * 3,
        out_specs=pl.BlockSpec((None, S, LANES), lambda b, hp: (b, 0, hp)),
        out_shape=jax.ShapeDtypeStruct((B, S, H * V_DIM), BF16),
        scratch_shapes=[s_buf, m_buf, s_buf, m_buf],
        compiler_params=_params(2),
        name="mla_attn",
    )(q, k, v)


def _pad_heads(w, width):
    kdim = w.shape[0]
    w = w.reshape(kdim, B_HEADS, width)
    w = jnp.pad(w, ((0, 0), (0, 0), (0, LANES - width)))
    return w.reshape(kdim, B_HEADS * LANES)


def kernel(x, positions, norm_mix, norm_ffn, a_w_qkv, a_sink, a_w_o, b_w_in, b_g_q, b_g_kv,
           b_w_uq, b_w_ukv, b_w_o, ffn_w_gu, ffn_w_down, final_norm):
    B, S, D = x.shape
    T = B * S
    h = x.reshape(T, D)
    pos_col3 = positions.reshape(B, S, 1)
    pos_blk3 = positions.reshape(B, S // BLK, BLK)

    half = ROPE_DIM // 2
    inv_freq = ROPE_THETA ** (-jnp.arange(half, dtype=F32) * 2.0 / ROPE_DIM)
    inv_lane = jnp.zeros((1, LANES), F32)
    inv_lane = inv_lane.at[0, NOPE_DIM:NOPE_DIM + half].set(inv_freq)
    inv_lane = inv_lane.at[0, NOPE_DIM + half:NOPE_DIM + ROPE_DIM].set(inv_freq)
    ct, s1, s2 = _rope_tables(positions.reshape(T, 1), inv_lane)
    ct, s1, s2 = (t.reshape(B, S, LANES) for t in (ct, s1, s2))

    gf = final_norm.reshape(1, D)
    for i in range(DEPTH):
        j = i // 2
        g_mix = norm_mix[i].reshape(1, D)
        if i % 2 == 0:
            q, k, v = _norm_qkv(h, g_mix, a_w_qkv[j].astype(BF16))
            nq = q.shape[1]
            nk = k.shape[1]
            attn = _win_attn(q.reshape(B, S, nq), k.reshape(B, S, nk), v.reshape(B, S, nk),
                             pos_col3, pos_blk3, a_sink[j])
            attn = attn.reshape(T, nq)
            w_o = a_w_o[j].astype(BF16)
        else:
            w_in = b_w_in[j]
            pad_k = jnp.zeros((D, NOPE_DIM), F32)
            pad_t = jnp.zeros((D, LANES - NOPE_DIM - ROPE_DIM), F32)
            w_in = jnp.concatenate([w_in[:, :Q_LORA + KV_LORA], pad_k,
                                    w_in[:, Q_LORA + KV_LORA:], pad_t], axis=1).astype(BF16)
            w_uq = _pad_heads(b_w_uq[j], NOPE_DIM + ROPE_DIM).astype(BF16)
            w_ukv = b_w_ukv[j].reshape(KV_LORA, B_HEADS, NOPE_DIM + V_DIM)
            w_uk = _pad_heads(w_ukv[:, :, :NOPE_DIM].reshape(KV_LORA, B_HEADS * NOPE_DIM),
                              NOPE_DIM).astype(BF16)
            w_uv = w_ukv[:, :, NOPE_DIM:].reshape(KV_LORA, B_HEADS * V_DIM).astype(BF16)
            qh, kh, v = _mla_prep(h.reshape(B, S, D), g_mix, w_in,
                                  b_g_q[j].reshape(1, Q_LORA), b_g_kv[j].reshape(1, KV_LORA),
                                  w_uq, w_uk, w_uv, ct, s1, s2)
            attn = _mla_attn(qh, kh, v).reshape(T, B_HEADS * V_DIM)
            w_o = b_w_o[j].astype(BF16)
        h = _proj_ffn(h, attn, w_o, norm_ffn[i].reshape(1, D),
                      ffn_w_gu[i].astype(BF16), ffn_w_down[i].astype(BF16),
                      gf, final_norm=(i == DEPTH - 1))
    return h.reshape(B, S, D)
```

```python
import functools
import math

import jax
import jax.numpy as jnp
from jax import lax
from jax.experimental import pallas as pl
from jax.experimental.pallas import tpu as pltpu

F32 = jnp.float32
BF16 = jnp.bfloat16

D_MODEL = 1024
DEPTH = 4
A_HEADS = 16
A_KV_HEADS = 4
A_GROUP = A_HEADS // A_KV_HEADS
A_HEAD_DIM = 64
WINDOW = 128
BLK = 128
B_HEADS = 16
Q_LORA = 384
KV_LORA = 256
NOPE_DIM = 64
ROPE_DIM = 32
V_DIM = 64
ROPE_THETA = 10000.0
D_FF = 2816
EPS = 1e-6

LOG2E = math.log2(math.e)
MASK_DIST = 1e36

LANES = 128
VMEM_LIMIT = 56 * 1024 * 1024

TM_PROJ = 512
TM_FFN = 512
TF_FFN = 256
TM_PREP = 512
SUB_PREP = 256

ROPE_HALF = ROPE_DIM // 2
ROPE_PARTNER = LANES // 2
TQ_MLA = 256

NT_DIMS = (((1,), (1,)), ((), ()))


def _rms(x, g):
    ms = jnp.mean(x * x, axis=-1, keepdims=True)
    return x * lax.rsqrt(ms + EPS) * g


def _params(n_axes):
    return pltpu.CompilerParams(dimension_semantics=("arbitrary",) * n_axes,
                                vmem_limit_bytes=VMEM_LIMIT)


def _norm_qkv_kernel(x_ref, g_ref, w_ref, q_ref, k_ref, v_ref):
    xn = _rms(x_ref[...], g_ref[...]).astype(BF16)
    qkv = jnp.dot(xn, w_ref[...], preferred_element_type=F32)
    nq = A_HEADS * A_HEAD_DIM
    nk = A_KV_HEADS * A_HEAD_DIM
    q_ref[...] = (qkv[:, :nq] * (A_HEAD_DIM ** -0.5 * LOG2E)).astype(BF16)
    k_ref[...] = qkv[:, nq:nq + nk].astype(BF16)
    v_ref[...] = qkv[:, nq + nk:].astype(BF16)


def _norm_qkv(x, g, w):
    T, D = x.shape
    N = w.shape[1]
    nq = A_HEADS * A_HEAD_DIM
    nk = A_KV_HEADS * A_HEAD_DIM
    tm = TM_PROJ
    return pl.pallas_call(
        _norm_qkv_kernel,
        grid=(T // tm,),
        in_specs=[pl.BlockSpec((tm, D), lambda i: (i, 0)),
                  pl.BlockSpec((1, D), lambda i: (0, 0)),
                  pl.BlockSpec((D, N), lambda i: (0, 0))],
        out_specs=[pl.BlockSpec((tm, nq), lambda i: (i, 0)),
                   pl.BlockSpec((tm, nk), lambda i: (i, 0)),
                   pl.BlockSpec((tm, nk), lambda i: (i, 0))],
        out_shape=[jax.ShapeDtypeStruct((T, nq), BF16),
                   jax.ShapeDtypeStruct((T, nk), BF16),
                   jax.ShapeDtypeStruct((T, nk), BF16)],
        compiler_params=_params(1),
        name="norm_qkv",
    )(x, g, w)


def _alibi_slope(h):
    return float(2.0 ** (-8.0 * (h + 1) / A_HEADS))


def _win_attn_kernel(sink_ref, q_ref, k_ref, v_ref, pc_ref, pr_ref, o_ref,
                     s0_sc, m0_sc, s1_sc, m1_sc, *, n_blocks):
    nkey = 3 * BLK
    lane = lax.broadcasted_iota(jnp.int32, (1, LANES), 1)
    low = lane < A_HEAD_DIM
    r = lax.broadcasted_iota(jnp.int32, (BLK, nkey), 0)
    c = lax.broadcasted_iota(jnp.int32, (BLK, nkey), 1)
    cr = c - r

    def window(blk):
        if isinstance(blk, int):
            return min(max(blk - 1, 0), n_blocks - 3)
        return jnp.clip(blk - 1, 0, n_blocks - 3)

    def row_start(blk):
        return blk * BLK if isinstance(blk, int) else pl.multiple_of(blk * BLK, BLK)

    def scores(blk, s_sc, m_sc):
        wb = window(blk)
        r0 = row_start(blk)
        w0 = row_start(wb)
        valid = jnp.abs(cr + (wb - blk) * BLK) <= WINDOW
        qpos = pc_ref[pl.ds(r0, BLK), :]
        kpos = jnp.concatenate([pr_ref[pl.ds(wb + t, 1), :] for t in range(3)], axis=1)
        dist = jnp.abs(qpos - kpos).astype(F32)
        dm = jnp.where(valid, dist, MASK_DIST)
        kk = k_ref[pl.ds(w0, nkey), :].astype(F32)
        for kh in range(A_KV_HEADS):
            pair = kh // 2
            kpair = kk[:, pair * LANES:(pair + 1) * LANES]
            kswap = pltpu.roll(kpair, A_HEAD_DIM, 1)
            k_own, k_other = (kpair, kswap) if kh % 2 == 0 else (kswap, kpair)
            k_lo = jnp.where(low, k_own, 0.0)
            k_hi = jnp.where(low, 0.0, k_other)
            kcat = jnp.concatenate([k_lo, k_hi], axis=0).astype(BF16)
            col = kh * A_GROUP * A_HEAD_DIM
            qst = jnp.concatenate([q_ref[pl.ds(r0, BLK), col:col + LANES],
                                   q_ref[pl.ds(r0, BLK), col + LANES:col + 2 * LANES]], axis=0)
            s2 = lax.dot_general(qst, kcat, NT_DIMS, preferred_element_type=F32)
            for g in range(A_GROUP):
                half, e = divmod(g, 2)
                h = kh * A_GROUP + g
                s = (s2[half * BLK:(half + 1) * BLK, e * nkey:(e + 1) * nkey]
                     - (_alibi_slope(h) * LOG2E) * dm)
                s_sc[h] = s
                m = jnp.maximum(jnp.max(s, axis=-1, keepdims=True), sink_ref[h] * LOG2E)
                m_sc[h] = jnp.broadcast_to(m, (BLK, LANES))

    def context(blk, s_sc, m_sc):
        r0 = row_start(blk)
        w0 = row_start(window(blk))
        vv = v_ref[pl.ds(w0, nkey), :].astype(F32)
        for kh in range(A_KV_HEADS):
            pair = kh // 2
            vpair = vv[:, pair * LANES:(pair + 1) * LANES]
            vswap = pltpu.roll(vpair, A_HEAD_DIM, 1)
            v_own, v_other = (vpair, vswap) if kh % 2 == 0 else (vswap, vpair)
            v_lo = jnp.where(low, v_own, 1.0).astype(BF16)
            v_hi = jnp.where(low, 1.0, v_other).astype(BF16)
            probs, sink_terms = [], []
            for g in range(A_GROUP):
                h = kh * A_GROUP + g
                m = m_sc[h]
                probs.append(jnp.exp2(s_sc[h] - jnp.tile(m, (1, 3))).astype(BF16))
                sink_terms.append(jnp.exp2(sink_ref[h] * LOG2E - m))
            o_even = jnp.dot(jnp.concatenate([probs[0], probs[2]], axis=0), v_lo,
                             preferred_element_type=F32)
            o_odd = jnp.dot(jnp.concatenate([probs[1], probs[3]], axis=0), v_hi,
                            preferred_element_type=F32)
            col = kh * A_GROUP * A_HEAD_DIM
            for half in range(A_GROUP // 2):
                oe = o_even[half * BLK:(half + 1) * BLK]
                oo = o_odd[half * BLK:(half + 1) * BLK]
                oe = oe / (pltpu.roll(oe, A_HEAD_DIM, 1) + sink_terms[2 * half])
                oo = oo / (pltpu.roll(oo, A_HEAD_DIM, 1) + sink_terms[2 * half + 1])
                o_ref[pl.ds(r0, BLK), col + half * LANES:col + (half + 1) * LANES] = (
                    jnp.where(low, oe, oo).astype(BF16))

    scores(0, s0_sc, m0_sc)

    def body(j, carry):
        context(2 * j, s0_sc, m0_sc)
        scores(2 * j + 1, s1_sc, m1_sc)
        context(2 * j + 1, s1_sc, m1_sc)
        scores(2 * j + 2, s0_sc, m0_sc)
        return carry

    lax.fori_loop(0, n_blocks // 2 - 1, body, 0)
    context(n_blocks - 2, s0_sc, m0_sc)
    scores(n_blocks - 1, s1_sc, m1_sc)
    context(n_blocks - 1, s1_sc, m1_sc)


def _win_attn(q, k, v, pos_col, pos_blk, sink):
    B, S, nq = q.shape
    nk = k.shape[2]
    nb = S // BLK
    whole = lambda b: (b, 0, 0)
    s_buf = pltpu.VMEM((A_HEADS, BLK, 3 * BLK), F32)
    m_buf = pltpu.VMEM((A_HEADS, BLK, LANES), F32)
    return pl.pallas_call(
        functools.partial(_win_attn_kernel, n_blocks=nb),
        grid=(B,),
        in_specs=[pl.BlockSpec(memory_space=pltpu.SMEM),
                  pl.BlockSpec((None, S, nq), whole),
                  pl.BlockSpec((None, S, nk), whole),
                  pl.BlockSpec((None, S, nk), whole),
                  pl.BlockSpec((None, S, 1), whole),
                  pl.BlockSpec((None, nb, BLK), whole)],
        out_specs=pl.BlockSpec((None, S, nq), whole),
        out_shape=jax.ShapeDtypeStruct((B, S, nq), BF16),
        scratch_shapes=[s_buf, m_buf, s_buf, m_buf],
        compiler_params=_params(1),
        name="win_attn",
    )(sink, q, k, v, pos_col, pos_blk)


def _proj_ffn_kernel(h_ref, a_ref, wo_ref, g_ref, wgu_ref, wd_ref, gf_ref, out_ref, act_sc,
                     *, final_norm):
    h1 = h_ref[...] + jnp.dot(a_ref[...], wo_ref[...], preferred_element_type=F32)
    xn = _rms(h1, g_ref[...]).astype(BF16)
    for c in range(D_FF // TF_FFN):
        lo = c * TF_FFN
        gate = jnp.dot(xn, wgu_ref[:, lo:lo + TF_FFN], preferred_element_type=F32)
        up = jnp.dot(xn, wgu_ref[:, D_FF + lo:D_FF + lo + TF_FFN], preferred_element_type=F32)
        act_sc[:, lo:lo + TF_FFN] = (gate * jax.nn.sigmoid(gate) * up).astype(BF16)
    y = h1 + jnp.dot(act_sc[...], wd_ref[...], preferred_element_type=F32)
    if final_norm:
        y = _rms(y, gf_ref[...])
    out_ref[...] = y


def _proj_ffn(h, a, wo, g, wgu, wd, gf, final_norm):
    T, D = h.shape
    tm = TM_FFN
    row = lambda i: (i, 0)
    const = lambda i: (0, 0)
    resident = pl.Buffered(1)
    return pl.pallas_call(
        functools.partial(_proj_ffn_kernel, final_norm=final_norm),
        grid=(T // tm,),
        in_specs=[pl.BlockSpec((tm, D), row),
                  pl.BlockSpec((tm, a.shape[1]), row),
                  pl.BlockSpec(wo.shape, const, pipeline_mode=resident),
                  pl.BlockSpec((1, D), const),
                  pl.BlockSpec(wgu.shape, const, pipeline_mode=resident),
                  pl.BlockSpec(wd.shape, const, pipeline_mode=resident),
                  pl.BlockSpec((1, D), const)],
        out_specs=pl.BlockSpec((tm, D), row),
        out_shape=jax.ShapeDtypeStruct((T, D), F32),
        scratch_shapes=[pltpu.VMEM((tm, D_FF), BF16)],
        compiler_params=_params(1),
        name="proj_ffn",
    )(h, a, wo, g, wgu, wd, gf)


def _rope_table_kernel(pos_ref, inv_ref, ct_ref, sg_ref):
    ang = pos_ref[...].astype(F32) * inv_ref[...]
    lane = lax.broadcasted_iota(jnp.int32, ang.shape, 1)
    first = lane < ROPE_HALF
    second = (lane >= ROPE_PARTNER) & (lane < ROPE_PARTNER + ROPE_HALF)
    sin = jnp.sin(ang)
    ct_ref[...] = jnp.where(first | second, jnp.cos(ang), 1.0)
    sg_ref[...] = jnp.where(first, -sin, jnp.where(second, sin, 0.0))


def _rope_tables(pos_col, inv_lane):
    T = pos_col.shape[0]
    tm = 2048
    tab = jax.ShapeDtypeStruct((T, LANES), F32)
    return pl.pallas_call(
        _rope_table_kernel,
        grid=(T // tm,),
        in_specs=[pl.BlockSpec((tm, 1), lambda i: (i, 0)),
                  pl.BlockSpec((1, LANES), lambda i: (0, 0))],
        out_specs=[pl.BlockSpec((tm, LANES), lambda i: (i, 0))] * 2,
        out_shape=[tab, tab],
        compiler_params=_params(1),
        name="rope_tables",
    )(pos_col, inv_lane)


def _mla_prep_kernel(x_ref, g_ref, win_ref, gq_ref, gkv_ref, wuq_ref, wuk_ref, wuv_ref,
                     ct_ref, sg_ref, q_ref, k_ref, v_ref):
    scale = (NOPE_DIM + ROPE_DIM) ** -0.5 * LOG2E
    low = lax.broadcasted_iota(jnp.int32, (1, LANES), 1) < V_DIM
    for part in range(TM_PREP // SUB_PREP):
        rows = slice(part * SUB_PREP, (part + 1) * SUB_PREP)
        xn = _rms(x_ref[rows, :], g_ref[...]).astype(BF16)
        lat = jnp.dot(xn, win_ref[...], preferred_element_type=F32)
        cq = _rms(lat[:, :Q_LORA], gq_ref[...]).astype(BF16)
        ckv = _rms(lat[:, Q_LORA:Q_LORA + KV_LORA], gkv_ref[...]).astype(BF16)
        ct = ct_ref[rows, :]
        sg = sg_ref[rows, :]
        k_rope = lat[:, Q_LORA + KV_LORA:]
        k_rope = k_rope * ct + pltpu.roll(k_rope, ROPE_PARTNER, 1) * sg
        ctq = ct * scale
        sgq = sg * scale
        q = jnp.dot(cq, wuq_ref[...], preferred_element_type=F32)
        kn = jnp.dot(ckv, wuk_ref[...], preferred_element_type=F32)
        v = jnp.dot(ckv, wuv_ref[...], preferred_element_type=F32)
        for h in range(B_HEADS):
            sl = slice(h * LANES, (h + 1) * LANES)
            qh = q[:, sl]
            q_ref[h, rows, :] = (qh * ctq + pltpu.roll(qh, ROPE_PARTNER, 1) * sgq).astype(BF16)
            k_ref[h, rows, :] = (kn[:, sl] + k_rope).astype(BF16)
            vpair = v[:, (h // 2) * LANES:(h // 2 + 1) * LANES]
            v_aug = jnp.where(low, vpair, 1.0) if h % 2 == 0 else jnp.where(low, 1.0, vpair)
            v_ref[h, rows, :] = v_aug.astype(BF16)


def _mla_prep(x, g, win, gq, gkv, wuq, wuk, wuv, ct, sg):
    B, S, D = x.shape
    tm = TM_PREP
    row = lambda b, i: (b, i, 0)
    const = lambda b, i: (0, 0)
    head = lambda b, i: (b, 0, i, 0)
    return pl.pallas_call(
        _mla_prep_kernel,
        grid=(B, S // tm),
        in_specs=[pl.BlockSpec((None, tm, D), row),
                  pl.BlockSpec(g.shape, const),
                  pl.BlockSpec(win.shape, const),
                  pl.BlockSpec(gq.shape, const),
                  pl.BlockSpec(gkv.shape, const),
                  pl.BlockSpec(wuq.shape, const),
                  pl.BlockSpec(wuk.shape, const),
                  pl.BlockSpec(wuv.shape, const),
                  pl.BlockSpec((None, tm, LANES), row),
                  pl.BlockSpec((None, tm, LANES), row)],
        out_specs=[pl.BlockSpec((None, B_HEADS, tm, LANES), head)] * 3,
        out_shape=[jax.ShapeDtypeStruct((B, B_HEADS, S, LANES), BF16)] * 3,
        compiler_params=_params(2),
        name="mla_prep",
    )(x, g, win, gq, gkv, wuq, wuk, wuv, ct, sg)


def _mla_attn_kernel(q_ref, k_ref, v_ref, o_ref, s0_sc, m0_sc, s1_sc, m1_sc):
    S = q_ref.shape[1]
    n = S // TQ_MLA
    low = lax.broadcasted_iota(jnp.int32, (1, LANES), 1) < V_DIM

    def scores(blk, s_sc, m_sc):
        r0 = pl.multiple_of(blk * TQ_MLA, TQ_MLA)
        for e in range(2):
            q = q_ref[e, pl.ds(r0, TQ_MLA), :]
            s = lax.dot_general(q, k_ref[e], NT_DIMS, preferred_element_type=F32)
            s_sc[e] = s
            m_sc[e] = jnp.broadcast_to(jnp.max(s, axis=-1, keepdims=True), (TQ_MLA, LANES))

    def context(blk, s_sc, m_sc):
        r0 = pl.multiple_of(blk * TQ_MLA, TQ_MLA)
        outs = []
        for e in range(2):
            p = jnp.exp2(s_sc[e] - jnp.tile(m_sc[e], (1, S // LANES))).astype(BF16)
            o = jnp.dot(p, v_ref[e], preferred_element_type=F32)
            outs.append(o / pltpu.roll(o, V_DIM, 1))
        o_ref[pl.ds(r0, TQ_MLA), :] = jnp.where(low, outs[0], outs[1]).astype(BF16)

    scores(0, s0_sc, m0_sc)

    def body(j, carry):
        context(2 * j, s0_sc, m0_sc)
        scores(2 * j + 1, s1_sc, m1_sc)
        context(2 * j + 1, s1_sc, m1_sc)
        scores(2 * j + 2, s0_sc, m0_sc)
        return carry

    lax.fori_loop(0, n // 2 - 1, body, 0)
    context(n - 2, s0_sc, m0_sc)
    scores(n - 1, s1_sc, m1_sc)
    context(n - 1, s1_sc, m1_sc)


def _mla_attn(q, k, v):
    B, H, S, _ = q.shape
    pair = lambda b, hp: (b, hp, 0, 0)
    s_buf = pltpu.VMEM((2, TQ_MLA, S), F32)
    m_buf = pltpu.VMEM((2, TQ_MLA, LANES), F32)
    return pl.pallas_call(
        _mla_attn_kernel,
        grid=(B, H // 2),
        in_specs=[pl.BlockSpec((None, 2, S, LANES), pair)] * 3,
        out_specs=pl.BlockSpec((None, S, LANES), lambda b, hp: (b, 0, hp)),
        out_shape=jax.ShapeDtypeStruct((B, S, H * V_DIM), BF16),
        scratch_shapes=[s_buf, m_buf, s_buf, m_buf],
        compiler_params=_params(2),
        name="mla_attn",
    )(q, k, v)


def _head_block(nope, rope):
    kdim, heads = nope.shape[:2]
    split = ROPE_PARTNER - ROPE_HALF
    pad = jnp.zeros((kdim, heads, LANES - NOPE_DIM - ROPE_DIM), nope.dtype)
    w = jnp.concatenate([rope[..., :ROPE_HALF], nope[..., :split],
                         rope[..., ROPE_HALF:], nope[..., split:], pad], axis=-1)
    return w.reshape(kdim, heads * LANES)


def kernel(x, positions, norm_mix, norm_ffn, a_w_qkv, a_sink, a_w_o, b_w_in, b_g_q, b_g_kv,
           b_w_uq, b_w_ukv, b_w_o, ffn_w_gu, ffn_w_down, final_norm):
    B, S, D = x.shape
    T = B * S
    h = x.reshape(T, D)
    pos_col3 = positions.reshape(B, S, 1)
    pos_blk3 = positions.reshape(B, S // BLK, BLK)

    inv_freq = ROPE_THETA ** (-jnp.arange(ROPE_HALF, dtype=F32) * 2.0 / ROPE_DIM)
    inv_lane = jnp.zeros((1, LANES), F32)
    inv_lane = inv_lane.at[0, :ROPE_HALF].set(inv_freq)
    inv_lane = inv_lane.at[0, ROPE_PARTNER:ROPE_PARTNER + ROPE_HALF].set(inv_freq)
    ct, sg = _rope_tables(positions.reshape(T, 1), inv_lane)
    ct, sg = (t.reshape(B, S, LANES) for t in (ct, sg))

    gf = final_norm.reshape(1, D)
    for i in range(DEPTH):
        j = i // 2
        g_mix = norm_mix[i].reshape(1, D)
        if i % 2 == 0:
            q, k, v = _norm_qkv(h, g_mix, a_w_qkv[j].astype(BF16))
            nq = q.shape[1]
            nk = k.shape[1]
            attn = _win_attn(q.reshape(B, S, nq), k.reshape(B, S, nk), v.reshape(B, S, nk),
                             pos_col3, pos_blk3, a_sink[j])
            attn = attn.reshape(T, nq)
            w_o = a_w_o[j].astype(BF16)
        else:
            w_in = b_w_in[j]
            n_lat = Q_LORA + KV_LORA
            w_kr = _head_block(jnp.zeros((D, 1, NOPE_DIM), F32),
                               w_in[:, n_lat:].reshape(D, 1, ROPE_DIM))
            w_in = jnp.concatenate([w_in[:, :n_lat], w_kr], axis=1).astype(BF16)
            w_uq = b_w_uq[j].reshape(Q_LORA, B_HEADS, NOPE_DIM + ROPE_DIM)
            w_uq = _head_block(w_uq[..., :NOPE_DIM], w_uq[..., NOPE_DIM:]).astype(BF16)
            w_ukv = b_w_ukv[j].reshape(KV_LORA, B_HEADS, NOPE_DIM + V_DIM)
            w_uk = _head_block(w_ukv[..., :NOPE_DIM],
                               jnp.zeros((KV_LORA, B_HEADS, ROPE_DIM), F32)).astype(BF16)
            w_uv = w_ukv[..., NOPE_DIM:].reshape(KV_LORA, B_HEADS * V_DIM).astype(BF16)
            qh, kh, v = _mla_prep(h.reshape(B, S, D), g_mix, w_in,
                                  b_g_q[j].reshape(1, Q_LORA), b_g_kv[j].reshape(1, KV_LORA),
                                  w_uq, w_uk, w_uv, ct, sg)
            attn = _mla_attn(qh, kh, v).reshape(T, B_HEADS * V_DIM)
            w_o = b_w_o[j].astype(BF16)
        h = _proj_ffn(h, attn, w_o, norm_ffn[i].reshape(1, D),
                      ffn_w_gu[i].astype(BF16), ffn_w_down[i].astype(BF16),
                      gf, final_norm=(i == DEPTH - 1))
    return h.reshape(B, S, D)
```

```python
import functools
import math

import jax
import jax.numpy as jnp
from jax import lax
from jax.experimental import pallas as pl
from jax.experimental.pallas import tpu as pltpu

F32 = jnp.float32
BF16 = jnp.bfloat16

D_MODEL = 1024
DEPTH = 4
A_HEADS = 16
A_KV_HEADS = 4
A_GROUP = A_HEADS // A_KV_HEADS
A_HEAD_DIM = 64
WINDOW = 128
BLK = 128
B_HEADS = 16
Q_LORA = 384
KV_LORA = 256
NOPE_DIM = 64
ROPE_DIM = 32
V_DIM = 64
ROPE_THETA = 10000.0
D_FF = 2816
EPS = 1e-6

LOG2E = math.log2(math.e)
MASK_DIST = 1e36

LANES = 128
VMEM_LIMIT = 56 * 1024 * 1024

TM_PROJ = 512
TM_FFN = 512
TF_FFN = 256
TM_PREP = 512
SUB_PREP = 256

ROPE_HALF = ROPE_DIM // 2
ROPE_PARTNER = LANES // 2
TQ_MLA = 256
KC_MLA = 256

NT_DIMS = (((1,), (1,)), ((), ()))


def _rms(x, g):
    ms = jnp.mean(x * x, axis=-1, keepdims=True)
    return x * lax.rsqrt(ms + EPS) * g


def _params(n_axes):
    return pltpu.CompilerParams(dimension_semantics=("arbitrary",) * n_axes,
                                vmem_limit_bytes=VMEM_LIMIT)


def _norm_qkv_kernel(x_ref, g_ref, w_ref, q_ref, k_ref, v_ref):
    xn = _rms(x_ref[...], g_ref[...]).astype(BF16)
    qkv = jnp.dot(xn, w_ref[...], preferred_element_type=F32)
    nq = A_HEADS * A_HEAD_DIM
    nk = A_KV_HEADS * A_HEAD_DIM
    q_ref[...] = (qkv[:, :nq] * (A_HEAD_DIM ** -0.5 * LOG2E)).astype(BF16)
    k_ref[...] = qkv[:, nq:nq + nk].astype(BF16)
    v_ref[...] = qkv[:, nq + nk:].astype(BF16)


def _norm_qkv(x, g, w):
    T, D = x.shape
    N = w.shape[1]
    nq = A_HEADS * A_HEAD_DIM
    nk = A_KV_HEADS * A_HEAD_DIM
    tm = TM_PROJ
    return pl.pallas_call(
        _norm_qkv_kernel,
        grid=(T // tm,),
        in_specs=[pl.BlockSpec((tm, D), lambda i: (i, 0)),
                  pl.BlockSpec((1, D), lambda i: (0, 0)),
                  pl.BlockSpec((D, N), lambda i: (0, 0))],
        out_specs=[pl.BlockSpec((tm, nq), lambda i: (i, 0)),
                   pl.BlockSpec((tm, nk), lambda i: (i, 0)),
                   pl.BlockSpec((tm, nk), lambda i: (i, 0))],
        out_shape=[jax.ShapeDtypeStruct((T, nq), BF16),
                   jax.ShapeDtypeStruct((T, nk), BF16),
                   jax.ShapeDtypeStruct((T, nk), BF16)],
        compiler_params=_params(1),
        name="norm_qkv",
    )(x, g, w)


def _alibi_slope(h):
    return float(2.0 ** (-8.0 * (h + 1) / A_HEADS))


def _win_attn_kernel(sink_ref, q_ref, k_ref, v_ref, pc_ref, pr_ref, o_ref,
                     s0_sc, m0_sc, s1_sc, m1_sc, *, n_blocks):
    nkey = 3 * BLK
    lane = lax.broadcasted_iota(jnp.int32, (1, LANES), 1)
    low = lane < A_HEAD_DIM
    r = lax.broadcasted_iota(jnp.int32, (BLK, nkey), 0)
    c = lax.broadcasted_iota(jnp.int32, (BLK, nkey), 1)
    cr = c - r

    def window(blk):
        if isinstance(blk, int):
            return min(max(blk - 1, 0), n_blocks - 3)
        return jnp.clip(blk - 1, 0, n_blocks - 3)

    def row_start(blk):
        return blk * BLK if isinstance(blk, int) else pl.multiple_of(blk * BLK, BLK)

    def scores(blk, s_sc, m_sc):
        wb = window(blk)
        r0 = row_start(blk)
        w0 = row_start(wb)
        valid = jnp.abs(cr + (wb - blk) * BLK) <= WINDOW
        qpos = pc_ref[pl.ds(r0, BLK), :]
        kpos = jnp.concatenate([pr_ref[pl.ds(wb + t, 1), :] for t in range(3)], axis=1)
        dist = jnp.abs(qpos - kpos).astype(F32)
        dm = jnp.where(valid, dist, MASK_DIST)
        kk = k_ref[pl.ds(w0, nkey), :].astype(F32)
        for kh in range(A_KV_HEADS):
            pair = kh // 2
            kpair = kk[:, pair * LANES:(pair + 1) * LANES]
            kswap = pltpu.roll(kpair, A_HEAD_DIM, 1)
            k_own, k_other = (kpair, kswap) if kh % 2 == 0 else (kswap, kpair)
            k_lo = jnp.where(low, k_own, 0.0)
            k_hi = jnp.where(low, 0.0, k_other)
            kcat = jnp.concatenate([k_lo, k_hi], axis=0).astype(BF16)
            col = kh * A_GROUP * A_HEAD_DIM
            qst = jnp.concatenate([q_ref[pl.ds(r0, BLK), col:col + LANES],
                                   q_ref[pl.ds(r0, BLK), col + LANES:col + 2 * LANES]], axis=0)
            s2 = lax.dot_general(qst, kcat, NT_DIMS, preferred_element_type=F32)
            for g in range(A_GROUP):
                half, e = divmod(g, 2)
                h = kh * A_GROUP + g
                s = (s2[half * BLK:(half + 1) * BLK, e * nkey:(e + 1) * nkey]
                     - (_alibi_slope(h) * LOG2E) * dm)
                s_sc[h] = s
                m = jnp.maximum(jnp.max(s, axis=-1, keepdims=True), sink_ref[h] * LOG2E)
                m_sc[h] = jnp.broadcast_to(m, (BLK, LANES))

    def context(blk, s_sc, m_sc):
        r0 = row_start(blk)
        w0 = row_start(window(blk))
        vv = v_ref[pl.ds(w0, nkey), :].astype(F32)
        for kh in range(A_KV_HEADS):
            pair = kh // 2
            vpair = vv[:, pair * LANES:(pair + 1) * LANES]
            vswap = pltpu.roll(vpair, A_HEAD_DIM, 1)
            v_own, v_other = (vpair, vswap) if kh % 2 == 0 else (vswap, vpair)
            v_lo = jnp.where(low, v_own, 1.0).astype(BF16)
            v_hi = jnp.where(low, 1.0, v_other).astype(BF16)
            probs, sink_terms = [], []
            for g in range(A_GROUP):
                h = kh * A_GROUP + g
                m = m_sc[h]
                probs.append(jnp.exp2(s_sc[h] - jnp.tile(m, (1, 3))).astype(BF16))
                sink_terms.append(jnp.exp2(sink_ref[h] * LOG2E - m))
            o_even = jnp.dot(jnp.concatenate([probs[0], probs[2]], axis=0), v_lo,
                             preferred_element_type=F32)
            o_odd = jnp.dot(jnp.concatenate([probs[1], probs[3]], axis=0), v_hi,
                            preferred_element_type=F32)
            col = kh * A_GROUP * A_HEAD_DIM
            for half in range(A_GROUP // 2):
                oe = o_even[half * BLK:(half + 1) * BLK]
                oo = o_odd[half * BLK:(half + 1) * BLK]
                oe = oe / (pltpu.roll(oe, A_HEAD_DIM, 1) + sink_terms[2 * half])
                oo = oo / (pltpu.roll(oo, A_HEAD_DIM, 1) + sink_terms[2 * half + 1])
                o_ref[pl.ds(r0, BLK), col + half * LANES:col + (half + 1) * LANES] = (
                    jnp.where(low, oe, oo).astype(BF16))

    scores(0, s0_sc, m0_sc)

    def body(j, carry):
        context(2 * j, s0_sc, m0_sc)
        scores(2 * j + 1, s1_sc, m1_sc)
        context(2 * j + 1, s1_sc, m1_sc)
        scores(2 * j + 2, s0_sc, m0_sc)
        return carry

    lax.fori_loop(0, n_blocks // 2 - 1, body, 0)
    context(n_blocks - 2, s0_sc, m0_sc)
    scores(n_blocks - 1, s1_sc, m1_sc)
    context(n_blocks - 1, s1_sc, m1_sc)


def _win_attn(q, k, v, pos_col, pos_blk, sink):
    B, S, nq = q.shape
    nk = k.shape[2]
    nb = S // BLK
    whole = lambda b: (b, 0, 0)
    s_buf = pltpu.VMEM((A_HEADS, BLK, 3 * BLK), F32)
    m_buf = pltpu.VMEM((A_HEADS, BLK, LANES), F32)
    return pl.pallas_call(
        functools.partial(_win_attn_kernel, n_blocks=nb),
        grid=(B,),
        in_specs=[pl.BlockSpec(memory_space=pltpu.SMEM),
                  pl.BlockSpec((None, S, nq), whole),
                  pl.BlockSpec((None, S, nk), whole),
                  pl.BlockSpec((None, S, nk), whole),
                  pl.BlockSpec((None, S, 1), whole),
                  pl.BlockSpec((None, nb, BLK), whole)],
        out_specs=pl.BlockSpec((None, S, nq), whole),
        out_shape=jax.ShapeDtypeStruct((B, S, nq), BF16),
        scratch_shapes=[s_buf, m_buf, s_buf, m_buf],
        compiler_params=_params(1),
        name="win_attn",
    )(sink, q, k, v, pos_col, pos_blk)


def _proj_ffn_kernel(h_ref, a_ref, wo_ref, g_ref, wgu_ref, wd_ref, gf_ref, out_ref, act_sc,
                     *, final_norm):
    h1 = h_ref[...] + jnp.dot(a_ref[...], wo_ref[...], preferred_element_type=F32)
    xn = _rms(h1, g_ref[...]).astype(BF16)
    for c in range(D_FF // TF_FFN):
        lo = c * TF_FFN
        gate = jnp.dot(xn, wgu_ref[:, lo:lo + TF_FFN], preferred_element_type=F32)
        up = jnp.dot(xn, wgu_ref[:, D_FF + lo:D_FF + lo + TF_FFN], preferred_element_type=F32)
        act_sc[:, lo:lo + TF_FFN] = (gate * jax.nn.sigmoid(gate) * up).astype(BF16)
    y = h1 + jnp.dot(act_sc[...], wd_ref[...], preferred_element_type=F32)
    if final_norm:
        y = _rms(y, gf_ref[...])
    out_ref[...] = y


def _proj_ffn(h, a, wo, g, wgu_all, wd_all, layer, gf, final_norm):
    T, D = h.shape
    tm = TM_FFN
    row = lambda i: (i, 0)
    const = lambda i: (0, 0)
    this_layer = lambda i: (layer, 0, 0)
    resident = pl.Buffered(1)
    return pl.pallas_call(
        functools.partial(_proj_ffn_kernel, final_norm=final_norm),
        grid=(T // tm,),
        in_specs=[pl.BlockSpec((tm, D), row),
                  pl.BlockSpec((tm, a.shape[1]), row),
                  pl.BlockSpec(wo.shape, const, pipeline_mode=resident),
                  pl.BlockSpec((1, D), const),
                  pl.BlockSpec((None,) + wgu_all.shape[1:], this_layer, pipeline_mode=resident),
                  pl.BlockSpec((None,) + wd_all.shape[1:], this_layer, pipeline_mode=resident),
                  pl.BlockSpec((1, D), const)],
        out_specs=pl.BlockSpec((tm, D), row),
        out_shape=jax.ShapeDtypeStruct((T, D), F32),
        scratch_shapes=[pltpu.VMEM((tm, D_FF), BF16)],
        compiler_params=_params(1),
        name="proj_ffn",
    )(h, a, wo, g, wgu_all, wd_all, gf)


def _rope_table_kernel(pos_ref, inv_ref, ct_ref, sg_ref):
    ang = pos_ref[...].astype(F32) * inv_ref[...]
    lane = lax.broadcasted_iota(jnp.int32, ang.shape, 1)
    first = lane < ROPE_HALF
    second = (lane >= ROPE_PARTNER) & (lane < ROPE_PARTNER + ROPE_HALF)
    sin = jnp.sin(ang)
    ct_ref[...] = jnp.where(first | second, jnp.cos(ang), 1.0)
    sg_ref[...] = jnp.where(first, -sin, jnp.where(second, sin, 0.0))


def _rope_tables(pos_col, inv_lane):
    T = pos_col.shape[0]
    tm = 2048
    tab = jax.ShapeDtypeStruct((T, LANES), F32)
    return pl.pallas_call(
        _rope_table_kernel,
        grid=(T // tm,),
        in_specs=[pl.BlockSpec((tm, 1), lambda i: (i, 0)),
                  pl.BlockSpec((1, LANES), lambda i: (0, 0))],
        out_specs=[pl.BlockSpec((tm, LANES), lambda i: (i, 0))] * 2,
        out_shape=[tab, tab],
        compiler_params=_params(1),
        name="rope_tables",
    )(pos_col, inv_lane)


def _mla_prep_kernel(x_ref, g_ref, win_ref, gq_ref, gkv_ref, wuq_ref, wuk_ref, wuvt_ref,
                     ct_ref, sg_ref, q_ref, k_ref, vt_ref):
    scale = (NOPE_DIM + ROPE_DIM) ** -0.5 * LOG2E
    for part in range(TM_PREP // SUB_PREP):
        rows = slice(part * SUB_PREP, (part + 1) * SUB_PREP)
        xn = _rms(x_ref[rows, :], g_ref[...]).astype(BF16)
        lat = jnp.dot(xn, win_ref[...], preferred_element_type=F32)
        cq = _rms(lat[:, :Q_LORA], gq_ref[...]).astype(BF16)
        ckv = _rms(lat[:, Q_LORA:Q_LORA + KV_LORA], gkv_ref[...]).astype(BF16)
        ct = ct_ref[rows, :]
        sg = sg_ref[rows, :]
        k_rope = lat[:, Q_LORA + KV_LORA:]
        k_rope = k_rope * ct + pltpu.roll(k_rope, ROPE_PARTNER, 1) * sg
        ctq = ct * scale
        sgq = sg * scale
        q = jnp.dot(cq, wuq_ref[...], preferred_element_type=F32)
        kn = jnp.dot(ckv, wuk_ref[...], preferred_element_type=F32)
        vt = lax.dot_general(wuvt_ref[...], ckv, NT_DIMS, preferred_element_type=F32)
        ones = jnp.ones((V_DIM, SUB_PREP), BF16)
        for h in range(B_HEADS):
            sl = slice(h * LANES, (h + 1) * LANES)
            qh = q[:, sl]
            q_ref[h, rows, :] = (qh * ctq + pltpu.roll(qh, ROPE_PARTNER, 1) * sgq).astype(BF16)
            k_ref[h, rows, :] = (kn[:, sl] + k_rope).astype(BF16)
            vt_ref[h, :V_DIM, rows] = vt[h * V_DIM:(h + 1) * V_DIM, :].astype(BF16)
            vt_ref[h, V_DIM:, rows] = ones


def _mla_prep(x, g, win, gq, gkv, wuq, wuk, wuvt, ct, sg):
    B, S, D = x.shape
    tm = TM_PREP
    row = lambda b, i: (b, i, 0)
    const = lambda b, i: (0, 0)
    head = lambda b, i: (b, 0, i, 0)
    head_t = lambda b, i: (b, 0, 0, i)
    qk_shape = jax.ShapeDtypeStruct((B, B_HEADS, S, LANES), BF16)
    return pl.pallas_call(
        _mla_prep_kernel,
        grid=(B, S // tm),
        in_specs=[pl.BlockSpec((None, tm, D), row),
                  pl.BlockSpec(g.shape, const),
                  pl.BlockSpec(win.shape, const),
                  pl.BlockSpec(gq.shape, const),
                  pl.BlockSpec(gkv.shape, const),
                  pl.BlockSpec(wuq.shape, const),
                  pl.BlockSpec(wuk.shape, const),
                  pl.BlockSpec(wuvt.shape, const),
                  pl.BlockSpec((None, tm, LANES), row),
                  pl.BlockSpec((None, tm, LANES), row)],
        out_specs=[pl.BlockSpec((None, B_HEADS, tm, LANES), head),
                   pl.BlockSpec((None, B_HEADS, tm, LANES), head),
                   pl.BlockSpec((None, B_HEADS, 2 * V_DIM, tm), head_t)],
        out_shape=[qk_shape, qk_shape,
                   jax.ShapeDtypeStruct((B, B_HEADS, 2 * V_DIM, S), BF16)],
        compiler_params=_params(2),
        name="mla_prep",
    )(x, g, win, gq, gkv, wuq, wuk, wuvt, ct, sg)


def _mla_attn_kernel(q_ref, k_ref, vt_ref, o_ref, s0_sc, m0_sc, s1_sc, m1_sc):
    S = q_ref.shape[1]
    n = S // TQ_MLA

    def row_start(blk):
        if isinstance(blk, int):
            return blk * TQ_MLA
        return pl.multiple_of(blk * TQ_MLA, TQ_MLA)

    def stage(ctx, sc, s_old, m_old, s_new, m_new):
        outs = []
        for e in range(2):
            if sc is not None:
                q = q_ref[e, pl.ds(row_start(sc), TQ_MLA), :]
                m_run = None
            if ctx is not None:
                m = m_old[e][0:1, :]
                acc = None
            for c in range(S // KC_MLA):
                keys = slice(c * KC_MLA, (c + 1) * KC_MLA)
                if ctx is not None:
                    pt = jnp.exp2(s_old[e, keys, :] - m).astype(BF16)
                    part = jnp.dot(vt_ref[e, :, keys], pt, preferred_element_type=F32)
                    acc = part if acc is None else acc + part
                if sc is not None:
                    st = lax.dot_general(k_ref[e, keys, :], q, NT_DIMS,
                                         preferred_element_type=F32)
                    s_new[e, keys, :] = st
                    m_chunk = jnp.max(st, axis=0, keepdims=True)
                    m_run = m_chunk if m_run is None else jnp.maximum(m_run, m_chunk)
            if sc is not None:
                m_new[e] = jnp.broadcast_to(m_run, (8, TQ_MLA))
            if ctx is not None:
                outs.append(acc[:V_DIM] / acc[V_DIM:V_DIM + 1])
        if ctx is not None:
            o_ref[pl.ds(row_start(ctx), TQ_MLA), :] = (
                jnp.concatenate(outs, axis=0).T.astype(BF16))

    stage(None, 0, None, None, s0_sc, m0_sc)

    def body(j, carry):
        stage(2 * j, 2 * j + 1, s0_sc, m0_sc, s1_sc, m1_sc)
        stage(2 * j + 1, 2 * j + 2, s1_sc, m1_sc, s0_sc, m0_sc)
        return carry

    lax.fori_loop(0, n // 2 - 1, body, 0)
    stage(n - 2, n - 1, s0_sc, m0_sc, s1_sc, m1_sc)
    stage(n - 1, None, s1_sc, m1_sc, None, None)


def _mla_attn(q, k, v):
    B, H, S, _ = q.shape
    pair = lambda b, hp: (b, hp, 0, 0)
    s_buf = pltpu.VMEM((2, S, TQ_MLA), F32)
    m_buf = pltpu.VMEM((2, 8, TQ_MLA), F32)
    return pl.pallas_call(
        _mla_attn_kernel,
        grid=(B, H // 2),
        in_specs=[pl.BlockSpec((None, 2, S, LANES), pair),
                  pl.BlockSpec((None, 2, S, LANES), pair),
                  pl.BlockSpec((None, 2, 2 * V_DIM, S), pair)],
        out_specs=pl.BlockSpec((None, S, LANES), lambda b, hp: (b, 0, hp)),
        out_shape=jax.ShapeDtypeStruct((B, S, H * V_DIM), BF16),
        scratch_shapes=[s_buf, m_buf, s_buf, m_buf],
        compiler_params=_params(2),
        name="mla_attn",
    )(q, k, v)


def _head_block(nope, rope):
    kdim, heads = nope.shape[:2]
    split = ROPE_PARTNER - ROPE_HALF
    pad = jnp.zeros((kdim, heads, LANES - NOPE_DIM - ROPE_DIM), nope.dtype)
    w = jnp.concatenate([rope[..., :ROPE_HALF], nope[..., :split],
                         rope[..., ROPE_HALF:], nope[..., split:], pad], axis=-1)
    return w.reshape(kdim, heads * LANES)


def kernel(x, positions, norm_mix, norm_ffn, a_w_qkv, a_sink, a_w_o, b_w_in, b_g_q, b_g_kv,
           b_w_uq, b_w_ukv, b_w_o, ffn_w_gu, ffn_w_down, final_norm):
    B, S, D = x.shape
    T = B * S
    h = x.reshape(T, D)
    pos_col3 = positions.reshape(B, S, 1)
    pos_blk3 = positions.reshape(B, S // BLK, BLK)

    inv_freq = ROPE_THETA ** (-jnp.arange(ROPE_HALF, dtype=F32) * 2.0 / ROPE_DIM)
    inv_lane = jnp.zeros((1, LANES), F32)
    inv_lane = inv_lane.at[0, :ROPE_HALF].set(inv_freq)
    inv_lane = inv_lane.at[0, ROPE_PARTNER:ROPE_PARTNER + ROPE_HALF].set(inv_freq)
    ct, sg = _rope_tables(positions.reshape(T, 1), inv_lane)
    ct, sg = (t.reshape(B, S, LANES) for t in (ct, sg))

    gf = final_norm.reshape(1, D)
    w_gu_all = ffn_w_gu.astype(BF16)
    w_down_all = ffn_w_down.astype(BF16)
    for i in range(DEPTH):
        j = i // 2
        g_mix = norm_mix[i].reshape(1, D)
        if i % 2 == 0:
            q, k, v = _norm_qkv(h, g_mix, a_w_qkv[j].astype(BF16))
            nq = q.shape[1]
            nk = k.shape[1]
            attn = _win_attn(q.reshape(B, S, nq), k.reshape(B, S, nk), v.reshape(B, S, nk),
                             pos_col3, pos_blk3, a_sink[j])
            attn = attn.reshape(T, nq)
            w_o = a_w_o[j].astype(BF16)
        else:
            w_in = b_w_in[j]
            n_lat = Q_LORA + KV_LORA
            w_kr = _head_block(jnp.zeros((D, 1, NOPE_DIM), F32),
                               w_in[:, n_lat:].reshape(D, 1, ROPE_DIM))
            w_in = jnp.concatenate([w_in[:, :n_lat], w_kr], axis=1).astype(BF16)
            w_uq = b_w_uq[j].reshape(Q_LORA, B_HEADS, NOPE_DIM + ROPE_DIM)
            w_uq = _head_block(w_uq[..., :NOPE_DIM], w_uq[..., NOPE_DIM:]).astype(BF16)
            w_ukv = b_w_ukv[j].reshape(KV_LORA, B_HEADS, NOPE_DIM + V_DIM)
            w_uk = _head_block(w_ukv[..., :NOPE_DIM],
                               jnp.zeros((KV_LORA, B_HEADS, ROPE_DIM), F32)).astype(BF16)
            w_uv = w_ukv[..., NOPE_DIM:].reshape(KV_LORA, B_HEADS * V_DIM).T.astype(BF16)
            qh, kh, v = _mla_prep(h.reshape(B, S, D), g_mix, w_in,
                                  b_g_q[j].reshape(1, Q_LORA), b_g_kv[j].reshape(1, KV_LORA),
                                  w_uq, w_uk, w_uv, ct, sg)
            attn = _mla_attn(qh, kh, v).reshape(T, B_HEADS * V_DIM)
            w_o = b_w_o[j].astype(BF16)
        h = _proj_ffn(h, attn, w_o, norm_ffn[i].reshape(1, D), w_gu_all, w_down_all, i,
                      gf, final_norm=(i == DEPTH - 1))
    return h.reshape(B, S, D)
```

```python
import functools
import math

import jax
import jax.numpy as jnp
from jax import lax
from jax.experimental import pallas as pl
from jax.experimental.pallas import tpu as pltpu

F32 = jnp.float32
BF16 = jnp.bfloat16

D_MODEL = 1024
DEPTH = 4
A_HEADS = 16
A_KV_HEADS = 4
A_GROUP = A_HEADS // A_KV_HEADS
A_HEAD_DIM = 64
WINDOW = 128
BLK = 128
B_HEADS = 16
Q_LORA = 384
KV_LORA = 256
NOPE_DIM = 64
ROPE_DIM = 32
V_DIM = 64
ROPE_THETA = 10000.0
D_FF = 2816
EPS = 1e-6

LOG2E = math.log2(math.e)
MASK_DIST = 1e36

LANES = 128
VMEM_LIMIT = 56 * 1024 * 1024

TM_PROJ = 512
TM_FFN = 512
TF_FFN = 256
TM_PREP = 512
SUB_PREP = 256

ROPE_HALF = ROPE_DIM // 2
ROPE_PARTNER = LANES // 2
TQ_MLA = 256
KC_MLA = 256

NT_DIMS = (((1,), (1,)), ((), ()))


def _rms(x, g):
    ms = jnp.mean(x * x, axis=-1, keepdims=True)
    return x * lax.rsqrt(ms + EPS) * g


def _params(n_axes):
    return pltpu.CompilerParams(dimension_semantics=("arbitrary",) * n_axes,
                                vmem_limit_bytes=VMEM_LIMIT)


def _norm_qkv_kernel(x_ref, g_ref, wqk_ref, wvt_ref, q_ref, k_ref, vt_ref):
    nq = A_HEADS * A_HEAD_DIM
    tm = x_ref.shape[0]
    xn = _rms(x_ref[...], g_ref[...]).astype(BF16)
    qk = jnp.dot(xn, wqk_ref[...], preferred_element_type=F32)
    q_ref[...] = (qk[:, :nq] * (A_HEAD_DIM ** -0.5 * LOG2E)).astype(BF16)
    low = lax.broadcasted_iota(jnp.int32, (1, LANES), 1) < A_HEAD_DIM
    for kh in range(A_KV_HEADS):
        pair = qk[:, nq + (kh // 2) * LANES:nq + (kh // 2 + 1) * LANES]
        swap = pltpu.roll(pair, A_HEAD_DIM, 1)
        own, other = (pair, swap) if kh % 2 == 0 else (swap, pair)
        k_ref[kh, 0] = jnp.where(low, own, 0.0).astype(BF16)
        k_ref[kh, 1] = jnp.where(low, 0.0, other).astype(BF16)
    vt = lax.dot_general(wvt_ref[...], xn, NT_DIMS, preferred_element_type=F32)
    ones = jnp.ones((A_HEAD_DIM, BLK), BF16)
    for kh in range(A_KV_HEADS):
        for tb in range(tm // BLK):
            vt_ref[kh, tb, :A_HEAD_DIM, :] = (
                vt[kh * A_HEAD_DIM:(kh + 1) * A_HEAD_DIM, tb * BLK:(tb + 1) * BLK].astype(BF16))
            vt_ref[kh, tb, A_HEAD_DIM:, :] = ones


def _norm_qkv(x, g, wqk, wvt):
    T, D = x.shape
    nq = A_HEADS * A_HEAD_DIM
    tm = TM_PROJ
    const = lambda i: (0, 0)
    return pl.pallas_call(
        _norm_qkv_kernel,
        grid=(T // tm,),
        in_specs=[pl.BlockSpec((tm, D), lambda i: (i, 0)),
                  pl.BlockSpec((1, D), const),
                  pl.BlockSpec(wqk.shape, const),
                  pl.BlockSpec(wvt.shape, const)],
        out_specs=[pl.BlockSpec((tm, nq), lambda i: (i, 0)),
                   pl.BlockSpec((A_KV_HEADS, 2, tm, LANES), lambda i: (0, 0, i, 0)),
                   pl.BlockSpec((A_KV_HEADS, tm // BLK, 2 * A_HEAD_DIM, BLK),
                                lambda i: (0, i, 0, 0))],
        out_shape=[jax.ShapeDtypeStruct((T, nq), BF16),
                   jax.ShapeDtypeStruct((A_KV_HEADS, 2, T, LANES), BF16),
                   jax.ShapeDtypeStruct((A_KV_HEADS, T // BLK, 2 * A_HEAD_DIM, BLK), BF16)],
        compiler_params=_params(1),
        name="norm_qkv",
    )(x, g, wqk, wvt)


def _alibi_slope(h):
    return float(2.0 ** (-8.0 * (h + 1) / A_HEADS))


def _win_attn_kernel(sink_ref, q_ref, k_ref, vt_ref, pc_ref, pr_ref, o_ref,
                     s0_sc, m0_sc, s1_sc, m1_sc, *, n_blocks):
    nkey = 3 * BLK
    r = lax.broadcasted_iota(jnp.int32, (nkey, BLK), 0)
    c = lax.broadcasted_iota(jnp.int32, (nkey, BLK), 1)
    rc = r - c

    def window(blk):
        if isinstance(blk, int):
            return min(max(blk - 1, 0), n_blocks - 3)
        return jnp.clip(blk - 1, 0, n_blocks - 3)

    def row_start(blk):
        return blk * BLK if isinstance(blk, int) else pl.multiple_of(blk * BLK, BLK)

    def masked_distance(blk):
        wb = window(blk)
        valid = jnp.abs(rc + (wb - blk) * BLK) <= WINDOW
        kpos = pc_ref[pl.ds(row_start(wb), nkey), :]
        qpos = pr_ref[pl.ds(blk, 1), :]
        return jnp.where(valid, jnp.abs(kpos - qpos).astype(F32), MASK_DIST)

    def scores(blk, kh, dm, s_sc, m_sc):
        r0 = row_start(blk)
        w0 = row_start(window(blk))
        col = kh * A_GROUP * A_HEAD_DIM
        qst = jnp.concatenate([q_ref[pl.ds(r0, BLK), col:col + LANES],
                               q_ref[pl.ds(r0, BLK), col + LANES:col + 2 * LANES]], axis=0)
        for part in range(2):
            st = lax.dot_general(k_ref[kh, part, pl.ds(w0, nkey), :], qst, NT_DIMS,
                                 preferred_element_type=F32)
            for half in range(2):
                h = kh * A_GROUP + 2 * half + part
                s = st[:, half * BLK:(half + 1) * BLK] - (_alibi_slope(h) * LOG2E) * dm
                s_sc[h] = s
                m = jnp.maximum(jnp.max(s, axis=0, keepdims=True), sink_ref[h] * LOG2E)
                m_sc[h] = jnp.broadcast_to(m, (8, BLK))

    def context(blk, kh, s_sc, m_sc):
        r0 = row_start(blk)
        wb = window(blk)
        col = kh * A_GROUP * A_HEAD_DIM
        vt = jnp.concatenate([vt_ref[kh, wb + t] for t in range(3)], axis=1)
        normed = [None] * A_GROUP
        for part in range(2):
            heads = [kh * A_GROUP + 2 * half + part for half in range(2)]
            pt = jnp.concatenate(
                [jnp.exp2(s_sc[h] - m_sc[h][0:1, :]).astype(BF16) for h in heads], axis=1)
            ot = jnp.dot(vt, pt, preferred_element_type=F32)
            for half, h in enumerate(heads):
                cols = slice(half * BLK, (half + 1) * BLK)
                l = ot[A_HEAD_DIM:A_HEAD_DIM + 1, cols] + jnp.exp2(
                    sink_ref[h] * LOG2E - m_sc[h][0:1, :])
                normed[2 * half + part] = ot[:A_HEAD_DIM, cols] / l
        for half in range(A_GROUP // 2):
            pair_t = jnp.concatenate([normed[2 * half], normed[2 * half + 1]], axis=0)
            o_ref[pl.ds(r0, BLK), col + half * LANES:col + (half + 1) * LANES] = (
                pair_t.T.astype(BF16))

    def stage(ctx, sc, s_old, m_old, s_new, m_new):
        dm = None if sc is None else masked_distance(sc)
        for kh in range(A_KV_HEADS):
            if ctx is not None:
                context(ctx, kh, s_old, m_old)
            if sc is not None:
                scores(sc, kh, dm, s_new, m_new)

    stage(None, 0, None, None, s0_sc, m0_sc)

    def body(j, carry):
        stage(2 * j, 2 * j + 1, s0_sc, m0_sc, s1_sc, m1_sc)
        stage(2 * j + 1, 2 * j + 2, s1_sc, m1_sc, s0_sc, m0_sc)
        return carry

    lax.fori_loop(0, n_blocks // 2 - 1, body, 0)
    stage(n_blocks - 2, n_blocks - 1, s0_sc, m0_sc, s1_sc, m1_sc)
    stage(n_blocks - 1, None, s1_sc, m1_sc, None, None)


def _win_attn(q, kcat, vt, pos_col, pos_blk, sink):
    B, S, nq = q.shape
    nb = S // BLK
    s_buf = pltpu.VMEM((A_HEADS, 3 * BLK, BLK), F32)
    m_buf = pltpu.VMEM((A_HEADS, 8, BLK), F32)
    return pl.pallas_call(
        functools.partial(_win_attn_kernel, n_blocks=nb),
        grid=(B,),
        in_specs=[pl.BlockSpec(memory_space=pltpu.SMEM),
                  pl.BlockSpec((None, S, nq), lambda b: (b, 0, 0)),
                  pl.BlockSpec((A_KV_HEADS, 2, S, LANES), lambda b: (0, 0, b, 0)),
                  pl.BlockSpec((A_KV_HEADS, nb, 2 * A_HEAD_DIM, BLK), lambda b: (0, b, 0, 0)),
                  pl.BlockSpec((None, S, 1), lambda b: (b, 0, 0)),
                  pl.BlockSpec((None, nb, BLK), lambda b: (b, 0, 0))],
        out_specs=pl.BlockSpec((None, S, nq), lambda b: (b, 0, 0)),
        out_shape=jax.ShapeDtypeStruct((B, S, nq), BF16),
        scratch_shapes=[s_buf, m_buf, s_buf, m_buf],
        compiler_params=_params(1),
        name="win_attn",
    )(sink, q, kcat, vt, pos_col, pos_blk)


def _proj_ffn_kernel(h_ref, a_ref, wo_ref, g_ref, wgu_ref, wd_ref, gf_ref, out_ref, act_sc,
                     *, final_norm):
    h1 = h_ref[...] + jnp.dot(a_ref[...], wo_ref[...], preferred_element_type=F32)
    xn = _rms(h1, g_ref[...]).astype(BF16)
    for c in range(D_FF // TF_FFN):
        lo = c * TF_FFN
        gate = jnp.dot(xn, wgu_ref[:, lo:lo + TF_FFN], preferred_element_type=F32)
        up = jnp.dot(xn, wgu_ref[:, D_FF + lo:D_FF + lo + TF_FFN], preferred_element_type=F32)
        act_sc[:, lo:lo + TF_FFN] = (gate * jax.nn.sigmoid(gate) * up).astype(BF16)
    y = h1 + jnp.dot(act_sc[...], wd_ref[...], preferred_element_type=F32)
    if final_norm:
        y = _rms(y, gf_ref[...])
    out_ref[...] = y


def _proj_ffn(h, a, wo, g, wgu_all, wd_all, layer, gf, final_norm):
    T, D = h.shape
    tm = TM_FFN
    row = lambda i: (i, 0)
    const = lambda i: (0, 0)
    this_layer = lambda i: (layer, 0, 0)
    resident = pl.Buffered(1)
    return pl.pallas_call(
        functools.partial(_proj_ffn_kernel, final_norm=final_norm),
        grid=(T // tm,),
        in_specs=[pl.BlockSpec((tm, D), row),
                  pl.BlockSpec((tm, a.shape[1]), row),
                  pl.BlockSpec(wo.shape, const, pipeline_mode=resident),
                  pl.BlockSpec((1, D), const),
                  pl.BlockSpec((None,) + wgu_all.shape[1:], this_layer, pipeline_mode=resident),
                  pl.BlockSpec((None,) + wd_all.shape[1:], this_layer, pipeline_mode=resident),
                  pl.BlockSpec((1, D), const)],
        out_specs=pl.BlockSpec((tm, D), row),
        out_shape=jax.ShapeDtypeStruct((T, D), F32),
        scratch_shapes=[pltpu.VMEM((tm, D_FF), BF16)],
        compiler_params=_params(1),
        name="proj_ffn",
    )(h, a, wo, g, wgu_all, wd_all, gf)


def _rope_table_kernel(pos_ref, inv_ref, ct_ref, sg_ref):
    ang = pos_ref[...].astype(F32) * inv_ref[...]
    lane = lax.broadcasted_iota(jnp.int32, ang.shape, 1)
    first = lane < ROPE_HALF
    second = (lane >= ROPE_PARTNER) & (lane < ROPE_PARTNER + ROPE_HALF)
    sin = jnp.sin(ang)
    ct_ref[...] = jnp.where(first | second, jnp.cos(ang), 1.0)
    sg_ref[...] = jnp.where(first, -sin, jnp.where(second, sin, 0.0))


def _rope_tables(pos_col, inv_lane):
    T = pos_col.shape[0]
    tm = 2048
    tab = jax.ShapeDtypeStruct((T, LANES), F32)
    return pl.pallas_call(
        _rope_table_kernel,
        grid=(T // tm,),
        in_specs=[pl.BlockSpec((tm, 1), lambda i: (i, 0)),
                  pl.BlockSpec((1, LANES), lambda i: (0, 0))],
        out_specs=[pl.BlockSpec((tm, LANES), lambda i: (i, 0))] * 2,
        out_shape=[tab, tab],
        compiler_params=_params(1),
        name="rope_tables",
    )(pos_col, inv_lane)


def _mla_prep_kernel(x_ref, g_ref, win_ref, gq_ref, gkv_ref, wuq_ref, wuk_ref, wuvt_ref,
                     ct_ref, sg_ref, q_ref, k_ref, vt_ref):
    scale = (NOPE_DIM + ROPE_DIM) ** -0.5 * LOG2E
    for part in range(TM_PREP // SUB_PREP):
        rows = slice(part * SUB_PREP, (part + 1) * SUB_PREP)
        xn = _rms(x_ref[rows, :], g_ref[...]).astype(BF16)
        lat = jnp.dot(xn, win_ref[...], preferred_element_type=F32)
        cq = _rms(lat[:, :Q_LORA], gq_ref[...]).astype(BF16)
        ckv = _rms(lat[:, Q_LORA:Q_LORA + KV_LORA], gkv_ref[...]).astype(BF16)
        ct = ct_ref[rows, :]
        sg = sg_ref[rows, :]
        k_rope = lat[:, Q_LORA + KV_LORA:]
        k_rope = k_rope * ct + pltpu.roll(k_rope, ROPE_PARTNER, 1) * sg
        ctq = ct * scale
        sgq = sg * scale
        q = jnp.dot(cq, wuq_ref[...], preferred_element_type=F32)
        kn = jnp.dot(ckv, wuk_ref[...], preferred_element_type=F32)
        vt = lax.dot_general(wuvt_ref[...], ckv, NT_DIMS, preferred_element_type=F32)
        ones = jnp.ones((V_DIM, SUB_PREP), BF16)
        for h in range(B_HEADS):
            sl = slice(h * LANES, (h + 1) * LANES)
            qh = q[:, sl]
            q_ref[h, rows, :] = (qh * ctq + pltpu.roll(qh, ROPE_PARTNER, 1) * sgq).astype(BF16)
            k_ref[h, rows, :] = (kn[:, sl] + k_rope).astype(BF16)
            vt_ref[h, :V_DIM, rows] = vt[h * V_DIM:(h + 1) * V_DIM, :].astype(BF16)
            vt_ref[h, V_DIM:, rows] = ones


def _mla_prep(x, g, win, gq, gkv, wuq, wuk, wuvt, ct, sg):
    B, S, D = x.shape
    tm = TM_PREP
    row = lambda b, i: (b, i, 0)
    const = lambda b, i: (0, 0)
    head = lambda b, i: (b, 0, i, 0)
    head_t = lambda b, i: (b, 0, 0, i)
    qk_shape = jax.ShapeDtypeStruct((B, B_HEADS, S, LANES), BF16)
    return pl.pallas_call(
        _mla_prep_kernel,
        grid=(B, S // tm),
        in_specs=[pl.BlockSpec((None, tm, D), row),
                  pl.BlockSpec(g.shape, const),
                  pl.BlockSpec(win.shape, const),
                  pl.BlockSpec(gq.shape, const),
                  pl.BlockSpec(gkv.shape, const),
                  pl.BlockSpec(wuq.shape, const),
                  pl.BlockSpec(wuk.shape, const),
                  pl.BlockSpec(wuvt.shape, const),
                  pl.BlockSpec((None, tm, LANES), row),
                  pl.BlockSpec((None, tm, LANES), row)],
        out_specs=[pl.BlockSpec((None, B_HEADS, tm, LANES), head),
                   pl.BlockSpec((None, B_HEADS, tm, LANES), head),
                   pl.BlockSpec((None, B_HEADS, 2 * V_DIM, tm), head_t)],
        out_shape=[qk_shape, qk_shape,
                   jax.ShapeDtypeStruct((B, B_HEADS, 2 * V_DIM, S), BF16)],
        compiler_params=_params(2),
        name="mla_prep",
    )(x, g, win, gq, gkv, wuq, wuk, wuvt, ct, sg)


def _mla_attn_kernel(q_ref, k_ref, vt_ref, o_ref, s0_sc, m0_sc, s1_sc, m1_sc, acc_sc):
    S = q_ref.shape[1]
    n = S // TQ_MLA

    def row_start(blk):
        if isinstance(blk, int):
            return blk * TQ_MLA
        return pl.multiple_of(blk * TQ_MLA, TQ_MLA)

    def finish(blk):
        outs = [acc_sc[e][:V_DIM] / acc_sc[e][V_DIM:V_DIM + 1] for e in range(2)]
        o_ref[pl.ds(row_start(blk), TQ_MLA), :] = jnp.concatenate(outs, axis=0).T.astype(BF16)

    def stage(fin, ctx, sc, s_old, m_old, s_new, m_new):
        if fin is not None:
            finish(fin)
        for e in range(2):
            if sc is not None:
                q = q_ref[e, pl.ds(row_start(sc), TQ_MLA), :]
                m_run = None
            if ctx is not None:
                m = m_old[e][0:1, :]
                acc = None
            for c in range(S // KC_MLA):
                keys = slice(c * KC_MLA, (c + 1) * KC_MLA)
                if ctx is not None:
                    pt = jnp.exp2(s_old[e, keys, :] - m).astype(BF16)
                    part = jnp.dot(vt_ref[e, :, keys], pt, preferred_element_type=F32)
                    acc = part if acc is None else acc + part
                if sc is not None:
                    st = lax.dot_general(k_ref[e, keys, :], q, NT_DIMS,
                                         preferred_element_type=F32)
                    s_new[e, keys, :] = st
                    m_chunk = jnp.max(st, axis=0, keepdims=True)
                    m_run = m_chunk if m_run is None else jnp.maximum(m_run, m_chunk)
            if sc is not None:
                m_new[e] = jnp.broadcast_to(m_run, (8, TQ_MLA))
            if ctx is not None:
                acc_sc[e] = acc

    stage(None, None, 0, None, None, s0_sc, m0_sc)
    stage(None, 0, 1, s0_sc, m0_sc, s1_sc, m1_sc)

    def body(j, carry):
        stage(2 * j, 2 * j + 1, 2 * j + 2, s1_sc, m1_sc, s0_sc, m0_sc)
        stage(2 * j + 1, 2 * j + 2, 2 * j + 3, s0_sc, m0_sc, s1_sc, m1_sc)
        return carry

    lax.fori_loop(0, n // 2 - 1, body, 0)
    stage(n - 2, n - 1, None, s1_sc, m1_sc, None, None)
    finish(n - 1)


def _mla_attn(q, k, v):
    B, H, S, _ = q.shape
    pair = lambda b, hp: (b, hp, 0, 0)
    s_buf = pltpu.VMEM((2, S, TQ_MLA), F32)
    m_buf = pltpu.VMEM((2, 8, TQ_MLA), F32)
    return pl.pallas_call(
        _mla_attn_kernel,
        grid=(B, H // 2),
        in_specs=[pl.BlockSpec((None, 2, S, LANES), pair),
                  pl.BlockSpec((None, 2, S, LANES), pair),
                  pl.BlockSpec((None, 2, 2 * V_DIM, S), pair)],
        out_specs=pl.BlockSpec((None, S, LANES), lambda b, hp: (b, 0, hp)),
        out_shape=jax.ShapeDtypeStruct((B, S, H * V_DIM), BF16),
        scratch_shapes=[s_buf, m_buf, s_buf, m_buf,
                        pltpu.VMEM((2, 2 * V_DIM, TQ_MLA), F32)],
        compiler_params=_params(2),
        name="mla_attn",
    )(q, k, v)


def _head_block(nope, rope):
    kdim, heads = nope.shape[:2]
    split = ROPE_PARTNER - ROPE_HALF
    pad = jnp.zeros((kdim, heads, LANES - NOPE_DIM - ROPE_DIM), nope.dtype)
    w = jnp.concatenate([rope[..., :ROPE_HALF], nope[..., :split],
                         rope[..., ROPE_HALF:], nope[..., split:], pad], axis=-1)
    return w.reshape(kdim, heads * LANES)


def kernel(x, positions, norm_mix, norm_ffn, a_w_qkv, a_sink, a_w_o, b_w_in, b_g_q, b_g_kv,
           b_w_uq, b_w_ukv, b_w_o, ffn_w_gu, ffn_w_down, final_norm):
    B, S, D = x.shape
    T = B * S
    h = x.reshape(T, D)
    pos_col3 = positions.reshape(B, S, 1)
    pos_blk3 = positions.reshape(B, S // BLK, BLK)

    inv_freq = ROPE_THETA ** (-jnp.arange(ROPE_HALF, dtype=F32) * 2.0 / ROPE_DIM)
    inv_lane = jnp.zeros((1, LANES), F32)
    inv_lane = inv_lane.at[0, :ROPE_HALF].set(inv_freq)
    inv_lane = inv_lane.at[0, ROPE_PARTNER:ROPE_PARTNER + ROPE_HALF].set(inv_freq)
    ct, sg = _rope_tables(positions.reshape(T, 1), inv_lane)
    ct, sg = (t.reshape(B, S, LANES) for t in (ct, sg))

    gf = final_norm.reshape(1, D)
    w_gu_all = ffn_w_gu.astype(BF16)
    w_down_all = ffn_w_down.astype(BF16)
    for i in range(DEPTH):
        j = i // 2
        g_mix = norm_mix[i].reshape(1, D)
        if i % 2 == 0:
            nq = A_HEADS * A_HEAD_DIM
            nqk = nq + A_KV_HEADS * A_HEAD_DIM
            w_qkv = a_w_qkv[j]
            q, kcat, vt = _norm_qkv(h, g_mix, w_qkv[:, :nqk].astype(BF16),
                                    w_qkv[:, nqk:].T.astype(BF16))
            attn = _win_attn(q.reshape(B, S, nq), kcat, vt, pos_col3, pos_blk3, a_sink[j])
            attn = attn.reshape(T, nq)
            w_o = a_w_o[j].astype(BF16)
        else:
            w_in = b_w_in[j]
            n_lat = Q_LORA + KV_LORA
            w_kr = _head_block(jnp.zeros((D, 1, NOPE_DIM), F32),
                               w_in[:, n_lat:].reshape(D, 1, ROPE_DIM))
            w_in = jnp.concatenate([w_in[:, :n_lat], w_kr], axis=1).astype(BF16)
            w_uq = b_w_uq[j].reshape(Q_LORA, B_HEADS, NOPE_DIM + ROPE_DIM)
            w_uq = _head_block(w_uq[..., :NOPE_DIM], w_uq[..., NOPE_DIM:]).astype(BF16)
            w_ukv = b_w_ukv[j].reshape(KV_LORA, B_HEADS, NOPE_DIM + V_DIM)
            w_uk = _head_block(w_ukv[..., :NOPE_DIM],
                               jnp.zeros((KV_LORA, B_HEADS, ROPE_DIM), F32)).astype(BF16)
            w_uv = w_ukv[..., NOPE_DIM:].reshape(KV_LORA, B_HEADS * V_DIM).T.astype(BF16)
            qh, kh, v = _mla_prep(h.reshape(B, S, D), g_mix, w_in,
                                  b_g_q[j].reshape(1, Q_LORA), b_g_kv[j].reshape(1, KV_LORA),
                                  w_uq, w_uk, w_uv, ct, sg)
            attn = _mla_attn(qh, kh, v).reshape(T, B_HEADS * V_DIM)
            w_o = b_w_o[j].astype(BF16)
        h = _proj_ffn(h, attn, w_o, norm_ffn[i].reshape(1, D), w_gu_all, w_down_all, i,
                      gf, final_norm=(i == DEPTH - 1))
    return h.reshape(B, S, D)
```

```python
import functools
import math

import jax
import jax.numpy as jnp
from jax import lax
from jax.experimental import pallas as pl
from jax.experimental.pallas import tpu as pltpu

F32 = jnp.float32
BF16 = jnp.bfloat16

D_MODEL = 1024
DEPTH = 4
A_HEADS = 16
A_KV_HEADS = 4
A_GROUP = A_HEADS // A_KV_HEADS
A_HEAD_DIM = 64
WINDOW = 128
BLK = 128
B_HEADS = 16
Q_LORA = 384
KV_LORA = 256
NOPE_DIM = 64
ROPE_DIM = 32
V_DIM = 64
ROPE_THETA = 10000.0
D_FF = 2816
EPS = 1e-6

LOG2E = math.log2(math.e)
MASK_DIST = 1e36

LANES = 128
VMEM_LIMIT = 56 * 1024 * 1024

TM_PROJ = 512
TM_FFN = 1024
TF_FFN = 256
TM_PREP = 512
SUB_PREP = 256

ROPE_HALF = ROPE_DIM // 2
ROPE_PARTNER = LANES // 2
TQ_MLA = 256
KC_MLA = 256
PAIRS_MLA = 2

NT_DIMS = (((1,), (1,)), ((), ()))


def _rms(x, g):
    ms = jnp.mean(x * x, axis=-1, keepdims=True)
    return x * lax.rsqrt(ms + EPS) * g


def _params(n_axes):
    return pltpu.CompilerParams(dimension_semantics=("arbitrary",) * n_axes,
                                vmem_limit_bytes=VMEM_LIMIT)


def _norm_qkv_kernel(x_ref, g_ref, wqk_ref, wvt_ref, q_ref, k_ref, vt_ref):
    nq = A_HEADS * A_HEAD_DIM
    tm = x_ref.shape[0]
    xn = _rms(x_ref[...], g_ref[...]).astype(BF16)
    qk = jnp.dot(xn, wqk_ref[...], preferred_element_type=F32)
    q_ref[...] = (qk[:, :nq] * (A_HEAD_DIM ** -0.5 * LOG2E)).astype(BF16)
    low = lax.broadcasted_iota(jnp.int32, (1, LANES), 1) < A_HEAD_DIM
    for kh in range(A_KV_HEADS):
        pair = qk[:, nq + (kh // 2) * LANES:nq + (kh // 2 + 1) * LANES]
        swap = pltpu.roll(pair, A_HEAD_DIM, 1)
        own, other = (pair, swap) if kh % 2 == 0 else (swap, pair)
        k_ref[kh, 0] = jnp.where(low, own, 0.0).astype(BF16)
        k_ref[kh, 1] = jnp.where(low, 0.0, other).astype(BF16)
    vt = lax.dot_general(wvt_ref[...], xn, NT_DIMS, preferred_element_type=F32)
    ones = jnp.ones((A_HEAD_DIM, BLK), BF16)
    for kh in range(A_KV_HEADS):
        for tb in range(tm // BLK):
            vt_ref[kh, tb, :A_HEAD_DIM, :] = (
                vt[kh * A_HEAD_DIM:(kh + 1) * A_HEAD_DIM, tb * BLK:(tb + 1) * BLK].astype(BF16))
            vt_ref[kh, tb, A_HEAD_DIM:, :] = ones


def _norm_qkv(x, g, wqk, wvt):
    T, D = x.shape
    nq = A_HEADS * A_HEAD_DIM
    tm = TM_PROJ
    const = lambda i: (0, 0)
    return pl.pallas_call(
        _norm_qkv_kernel,
        grid=(T // tm,),
        in_specs=[pl.BlockSpec((tm, D), lambda i: (i, 0)),
                  pl.BlockSpec((1, D), const),
                  pl.BlockSpec(wqk.shape, const),
                  pl.BlockSpec(wvt.shape, const)],
        out_specs=[pl.BlockSpec((tm, nq), lambda i: (i, 0)),
                   pl.BlockSpec((A_KV_HEADS, 2, tm, LANES), lambda i: (0, 0, i, 0)),
                   pl.BlockSpec((A_KV_HEADS, tm // BLK, 2 * A_HEAD_DIM, BLK),
                                lambda i: (0, i, 0, 0))],
        out_shape=[jax.ShapeDtypeStruct((T, nq), BF16),
                   jax.ShapeDtypeStruct((A_KV_HEADS, 2, T, LANES), BF16),
                   jax.ShapeDtypeStruct((A_KV_HEADS, T // BLK, 2 * A_HEAD_DIM, BLK), BF16)],
        compiler_params=_params(1),
        name="norm_qkv",
    )(x, g, wqk, wvt)


def _alibi_slope(h):
    return float(2.0 ** (-8.0 * (h + 1) / A_HEADS))


def _win_attn_kernel(sink_ref, q_ref, k_ref, vt_ref, pc_ref, pr_ref, o_ref,
                     s0_sc, m0_sc, s1_sc, m1_sc, *, n_blocks):
    nkey = 3 * BLK
    r = lax.broadcasted_iota(jnp.int32, (nkey, BLK), 0)
    c = lax.broadcasted_iota(jnp.int32, (nkey, BLK), 1)
    rc = r - c

    def window(blk):
        if isinstance(blk, int):
            return min(max(blk - 1, 0), n_blocks - 3)
        return jnp.clip(blk - 1, 0, n_blocks - 3)

    def row_start(blk):
        return blk * BLK if isinstance(blk, int) else pl.multiple_of(blk * BLK, BLK)

    def masked_distance(blk):
        wb = window(blk)
        valid = jnp.abs(rc + (wb - blk) * BLK) <= WINDOW
        kpos = pc_ref[pl.ds(row_start(wb), nkey), :]
        qpos = pr_ref[pl.ds(blk, 1), :]
        return jnp.where(valid, jnp.abs(kpos - qpos).astype(F32), MASK_DIST)

    def scores(blk, kh, dm, s_sc, m_sc):
        r0 = row_start(blk)
        w0 = row_start(window(blk))
        col = kh * A_GROUP * A_HEAD_DIM
        qst = jnp.concatenate([q_ref[pl.ds(r0, BLK), col:col + LANES],
                               q_ref[pl.ds(r0, BLK), col + LANES:col + 2 * LANES]], axis=0)
        for part in range(2):
            st = lax.dot_general(k_ref[kh, part, pl.ds(w0, nkey), :], qst, NT_DIMS,
                                 preferred_element_type=F32)
            for half in range(2):
                h = kh * A_GROUP + 2 * half + part
                s = st[:, half * BLK:(half + 1) * BLK] - (_alibi_slope(h) * LOG2E) * dm
                s_sc[h] = s
                m = jnp.maximum(jnp.max(s, axis=0, keepdims=True), sink_ref[h] * LOG2E)
                m_sc[h] = jnp.broadcast_to(m, (8, BLK))

    def context(blk, kh, s_sc, m_sc):
        r0 = row_start(blk)
        wb = window(blk)
        col = kh * A_GROUP * A_HEAD_DIM
        vt = jnp.concatenate([vt_ref[kh, wb + t] for t in range(3)], axis=1)
        normed = [None] * A_GROUP
        for part in range(2):
            heads = [kh * A_GROUP + 2 * half + part for half in range(2)]
            pt = jnp.concatenate(
                [jnp.exp2(s_sc[h] - m_sc[h][0:1, :]).astype(BF16) for h in heads], axis=1)
            ot = jnp.dot(vt, pt, preferred_element_type=F32)
            for half, h in enumerate(heads):
                cols = slice(half * BLK, (half + 1) * BLK)
                l = ot[A_HEAD_DIM:A_HEAD_DIM + 1, cols] + jnp.exp2(
                    sink_ref[h] * LOG2E - m_sc[h][0:1, :])
                normed[2 * half + part] = ot[:A_HEAD_DIM, cols] / l
        for half in range(A_GROUP // 2):
            pair_t = jnp.concatenate([normed[2 * half], normed[2 * half + 1]], axis=0)
            o_ref[pl.ds(r0, BLK), col + half * LANES:col + (half + 1) * LANES] = (
                pair_t.T.astype(BF16))

    def stage(ctx, sc, s_old, m_old, s_new, m_new):
        dm = None if sc is None else masked_distance(sc)
        for kh in range(A_KV_HEADS):
            if ctx is not None:
                context(ctx, kh, s_old, m_old)
            if sc is not None:
                scores(sc, kh, dm, s_new, m_new)

    stage(None, 0, None, None, s0_sc, m0_sc)

    def body(j, carry):
        stage(2 * j, 2 * j + 1, s0_sc, m0_sc, s1_sc, m1_sc)
        stage(2 * j + 1, 2 * j + 2, s1_sc, m1_sc, s0_sc, m0_sc)
        return carry

    lax.fori_loop(0, n_blocks // 2 - 1, body, 0)
    stage(n_blocks - 2, n_blocks - 1, s0_sc, m0_sc, s1_sc, m1_sc)
    stage(n_blocks - 1, None, s1_sc, m1_sc, None, None)


def _win_attn(q, kcat, vt, pos_col, pos_blk, sink):
    B, S, nq = q.shape
    nb = S // BLK
    s_buf = pltpu.VMEM((A_HEADS, 3 * BLK, BLK), F32)
    m_buf = pltpu.VMEM((A_HEADS, 8, BLK), F32)
    return pl.pallas_call(
        functools.partial(_win_attn_kernel, n_blocks=nb),
        grid=(B,),
        in_specs=[pl.BlockSpec(memory_space=pltpu.SMEM),
                  pl.BlockSpec((None, S, nq), lambda b: (b, 0, 0)),
                  pl.BlockSpec((A_KV_HEADS, 2, S, LANES), lambda b: (0, 0, b, 0)),
                  pl.BlockSpec((A_KV_HEADS, nb, 2 * A_HEAD_DIM, BLK), lambda b: (0, b, 0, 0)),
                  pl.BlockSpec((None, S, 1), lambda b: (b, 0, 0)),
                  pl.BlockSpec((None, nb, BLK), lambda b: (b, 0, 0))],
        out_specs=pl.BlockSpec((None, S, nq), lambda b: (b, 0, 0)),
        out_shape=jax.ShapeDtypeStruct((B, S, nq), BF16),
        scratch_shapes=[s_buf, m_buf, s_buf, m_buf],
        compiler_params=_params(1),
        name="win_attn",
    )(sink, q, kcat, vt, pos_col, pos_blk)


def _proj_ffn_kernel(h_ref, a_ref, wo_ref, g_ref, wgu_ref, wd_ref, gf_ref, out_ref, act_sc,
                     *, final_norm):
    h1 = h_ref[...] + jnp.dot(a_ref[...], wo_ref[...], preferred_element_type=F32)
    xn = _rms(h1, g_ref[...]).astype(BF16)
    for c in range(D_FF // TF_FFN):
        lo = c * TF_FFN
        gate = jnp.dot(xn, wgu_ref[:, lo:lo + TF_FFN], preferred_element_type=F32)
        up = jnp.dot(xn, wgu_ref[:, D_FF + lo:D_FF + lo + TF_FFN], preferred_element_type=F32)
        act_sc[:, lo:lo + TF_FFN] = (gate * jax.nn.sigmoid(gate) * up).astype(BF16)
    y = h1 + jnp.dot(act_sc[...], wd_ref[...], preferred_element_type=F32)
    if final_norm:
        y = _rms(y, gf_ref[...])
    out_ref[...] = y


def _proj_ffn(h, a, wo, g, wgu_all, wd_all, layer, gf, final_norm):
    T, D = h.shape
    tm = TM_FFN
    row = lambda i: (i, 0)
    const = lambda i: (0, 0)
    this_layer = lambda i: (layer, 0, 0)
    resident = pl.Buffered(1)
    return pl.pallas_call(
        functools.partial(_proj_ffn_kernel, final_norm=final_norm),
        grid=(T // tm,),
        in_specs=[pl.BlockSpec((tm, D), row),
                  pl.BlockSpec((tm, a.shape[1]), row),
                  pl.BlockSpec(wo.shape, const, pipeline_mode=resident),
                  pl.BlockSpec((1, D), const),
                  pl.BlockSpec((None,) + wgu_all.shape[1:], this_layer, pipeline_mode=resident),
                  pl.BlockSpec((None,) + wd_all.shape[1:], this_layer, pipeline_mode=resident),
                  pl.BlockSpec((1, D), const)],
        out_specs=pl.BlockSpec((tm, D), row),
        out_shape=jax.ShapeDtypeStruct((T, D), F32),
        scratch_shapes=[pltpu.VMEM((tm, D_FF), BF16)],
        compiler_params=_params(1),
        name="proj_ffn",
    )(h, a, wo, g, wgu_all, wd_all, gf)


def _rope_table_kernel(pos_ref, inv_ref, ct_ref, sg_ref):
    ang = pos_ref[...].astype(F32) * inv_ref[...]
    lane = lax.broadcasted_iota(jnp.int32, ang.shape, 1)
    first = lane < ROPE_HALF
    second = (lane >= ROPE_PARTNER) & (lane < ROPE_PARTNER + ROPE_HALF)
    sin = jnp.sin(ang)
    ct_ref[...] = jnp.where(first | second, jnp.cos(ang), 1.0)
    sg_ref[...] = jnp.where(first, -sin, jnp.where(second, sin, 0.0))


def _rope_tables(pos_col, inv_lane):
    T = pos_col.shape[0]
    tm = 2048
    tab = jax.ShapeDtypeStruct((T, LANES), F32)
    return pl.pallas_call(
        _rope_table_kernel,
        grid=(T // tm,),
        in_specs=[pl.BlockSpec((tm, 1), lambda i: (i, 0)),
                  pl.BlockSpec((1, LANES), lambda i: (0, 0))],
        out_specs=[pl.BlockSpec((tm, LANES), lambda i: (i, 0))] * 2,
        out_shape=[tab, tab],
        compiler_params=_params(1),
        name="rope_tables",
    )(pos_col, inv_lane)


def _mla_prep_kernel(x_ref, g_ref, win_ref, gq_ref, gkv_ref, wuq_ref, wuk_ref, wuvt_ref,
                     ct_ref, sg_ref, q_ref, k_ref, vt_ref):
    scale = (NOPE_DIM + ROPE_DIM) ** -0.5 * LOG2E
    for part in range(TM_PREP // SUB_PREP):
        rows = slice(part * SUB_PREP, (part + 1) * SUB_PREP)
        xn = _rms(x_ref[rows, :], g_ref[...]).astype(BF16)
        lat = jnp.dot(xn, win_ref[...], preferred_element_type=F32)
        cq = _rms(lat[:, :Q_LORA], gq_ref[...]).astype(BF16)
        ckv = _rms(lat[:, Q_LORA:Q_LORA + KV_LORA], gkv_ref[...]).astype(BF16)
        ct = ct_ref[rows, :]
        sg = sg_ref[rows, :]
        k_rope = lat[:, Q_LORA + KV_LORA:]
        k_rope = k_rope * ct + pltpu.roll(k_rope, ROPE_PARTNER, 1) * sg
        ctq = ct * scale
        sgq = sg * scale
        q = jnp.dot(cq, wuq_ref[...], preferred_element_type=F32)
        kn = jnp.dot(ckv, wuk_ref[...], preferred_element_type=F32)
        vt = lax.dot_general(wuvt_ref[...], ckv, NT_DIMS, preferred_element_type=F32)
        ones = jnp.ones((V_DIM, SUB_PREP), BF16)
        for h in range(B_HEADS):
            sl = slice(h * LANES, (h + 1) * LANES)
            qh = q[:, sl]
            q_ref[h, rows, :] = (qh * ctq + pltpu.roll(qh, ROPE_PARTNER, 1) * sgq).astype(BF16)
            k_ref[h, rows, :] = (kn[:, sl] + k_rope).astype(BF16)
            vt_ref[h, :V_DIM, rows] = vt[h * V_DIM:(h + 1) * V_DIM, :].astype(BF16)
            vt_ref[h, V_DIM:, rows] = ones


def _mla_prep(x, g, win, gq, gkv, wuq, wuk, wuvt, ct, sg):
    B, S, D = x.shape
    tm = TM_PREP
    row = lambda b, i: (b, i, 0)
    const = lambda b, i: (0, 0)
    head = lambda b, i: (b, 0, i, 0)
    head_t = lambda b, i: (b, 0, 0, i)
    qk_shape = jax.ShapeDtypeStruct((B, B_HEADS, S, LANES), BF16)
    return pl.pallas_call(
        _mla_prep_kernel,
        grid=(B, S // tm),
        in_specs=[pl.BlockSpec((None, tm, D), row),
                  pl.BlockSpec(g.shape, const),
                  pl.BlockSpec(win.shape, const),
                  pl.BlockSpec(gq.shape, const),
                  pl.BlockSpec(gkv.shape, const),
                  pl.BlockSpec(wuq.shape, const),
                  pl.BlockSpec(wuk.shape, const),
                  pl.BlockSpec(wuvt.shape, const),
                  pl.BlockSpec((None, tm, LANES), row),
                  pl.BlockSpec((None, tm, LANES), row)],
        out_specs=[pl.BlockSpec((None, B_HEADS, tm, LANES), head),
                   pl.BlockSpec((None, B_HEADS, tm, LANES), head),
                   pl.BlockSpec((None, B_HEADS, 2 * V_DIM, tm), head_t)],
        out_shape=[qk_shape, qk_shape,
                   jax.ShapeDtypeStruct((B, B_HEADS, 2 * V_DIM, S), BF16)],
        compiler_params=_params(2),
        name="mla_prep",
    )(x, g, win, gq, gkv, wuq, wuk, wuvt, ct, sg)


def _mla_attn_kernel(q_ref, k_ref, vt_ref, o_ref, s0_sc, m0_sc, s1_sc, m1_sc, acc_sc):
    S = q_ref.shape[1]
    n = S // TQ_MLA

    def row_start(blk):
        if isinstance(blk, int):
            return blk * TQ_MLA
        return pl.multiple_of(blk * TQ_MLA, TQ_MLA)

    def finish(item):
        hp, blk = item
        outs = [acc_sc[e][:V_DIM] / acc_sc[e][V_DIM:V_DIM + 1] for e in range(2)]
        o_ref[pl.ds(row_start(blk), TQ_MLA), hp * LANES:(hp + 1) * LANES] = (
            jnp.concatenate(outs, axis=0).T.astype(BF16))

    def stage(fin, ctx, sc, s_old, m_old, s_new, m_new):
        if fin is not None:
            finish(fin)
        for e in range(2):
            if sc is not None:
                head_sc = 2 * sc[0] + e
                q = q_ref[head_sc, pl.ds(row_start(sc[1]), TQ_MLA), :]
                m_run = None
            if ctx is not None:
                head_ctx = 2 * ctx[0] + e
                m = m_old[e][0:1, :]
                acc = None
            for c in range(S // KC_MLA):
                keys = slice(c * KC_MLA, (c + 1) * KC_MLA)
                if ctx is not None:
                    pt = jnp.exp2(s_old[e, keys, :] - m).astype(BF16)
                    part = jnp.dot(vt_ref[head_ctx, :, keys], pt, preferred_element_type=F32)
                    acc = part if acc is None else acc + part
                if sc is not None:
                    st = lax.dot_general(k_ref[head_sc, keys, :], q, NT_DIMS,
                                         preferred_element_type=F32)
                    s_new[e, keys, :] = st
                    m_chunk = jnp.max(st, axis=0, keepdims=True)
                    m_run = m_chunk if m_run is None else jnp.maximum(m_run, m_chunk)
            if sc is not None:
                m_new[e] = jnp.broadcast_to(m_run, (8, TQ_MLA))
            if ctx is not None:
                acc_sc[e] = acc

    n_items = PAIRS_MLA * n
    scratch = ((s0_sc, m0_sc), (s1_sc, m1_sc))

    def item(t):
        return (t // n, t % n) if 0 <= t < n_items else None

    def static_stage(k):
        stage(item(k - 2), item(k - 1), item(k), *scratch[(k - 1) % 2], *scratch[k % 2])

    for hp in range(PAIRS_MLA):
        static_stage(hp * n)
        static_stage(hp * n + 1)

        def body(j, carry, hp=hp):
            stage((hp, 2 * j), (hp, 2 * j + 1), (hp, 2 * j + 2), *scratch[1], *scratch[0])
            stage((hp, 2 * j + 1), (hp, 2 * j + 2), (hp, 2 * j + 3), *scratch[0], *scratch[1])
            return carry

        lax.fori_loop(0, n // 2 - 1, body, 0)
    static_stage(n_items)
    static_stage(n_items + 1)


def _mla_attn(q, k, v):
    B, H, S, _ = q.shape
    heads = 2 * PAIRS_MLA
    pair = lambda b, g: (b, g, 0, 0)
    s_buf = pltpu.VMEM((2, S, TQ_MLA), F32)
    m_buf = pltpu.VMEM((2, 8, TQ_MLA), F32)
    return pl.pallas_call(
        _mla_attn_kernel,
        grid=(B, H // heads),
        in_specs=[pl.BlockSpec((None, heads, S, LANES), pair),
                  pl.BlockSpec((None, heads, S, LANES), pair),
                  pl.BlockSpec((None, heads, 2 * V_DIM, S), pair)],
        out_specs=pl.BlockSpec((None, S, PAIRS_MLA * LANES), lambda b, g: (b, 0, g)),
        out_shape=jax.ShapeDtypeStruct((B, S, H * V_DIM), BF16),
        scratch_shapes=[s_buf, m_buf, s_buf, m_buf,
                        pltpu.VMEM((2, 2 * V_DIM, TQ_MLA), F32)],
        compiler_params=_params(2),
        name="mla_attn",
    )(q, k, v)


def _head_block(nope, rope):
    kdim, heads = nope.shape[:2]
    split = ROPE_PARTNER - ROPE_HALF
    pad = jnp.zeros((kdim, heads, LANES - NOPE_DIM - ROPE_DIM), nope.dtype)
    w = jnp.concatenate([rope[..., :ROPE_HALF], nope[..., :split],
                         rope[..., ROPE_HALF:], nope[..., split:], pad], axis=-1)
    return w.reshape(kdim, heads * LANES)


def kernel(x, positions, norm_mix, norm_ffn, a_w_qkv, a_sink, a_w_o, b_w_in, b_g_q, b_g_kv,
           b_w_uq, b_w_ukv, b_w_o, ffn_w_gu, ffn_w_down, final_norm):
    B, S, D = x.shape
    T = B * S
    h = x.reshape(T, D)
    pos_col3 = positions.reshape(B, S, 1)
    pos_blk3 = positions.reshape(B, S // BLK, BLK)

    inv_freq = ROPE_THETA ** (-jnp.arange(ROPE_HALF, dtype=F32) * 2.0 / ROPE_DIM)
    inv_lane = jnp.zeros((1, LANES), F32)
    inv_lane = inv_lane.at[0, :ROPE_HALF].set(inv_freq)
    inv_lane = inv_lane.at[0, ROPE_PARTNER:ROPE_PARTNER + ROPE_HALF].set(inv_freq)
    ct, sg = _rope_tables(positions.reshape(T, 1), inv_lane)
    ct, sg = (t.reshape(B, S, LANES) for t in (ct, sg))

    gf = final_norm.reshape(1, D)
    w_gu_all = ffn_w_gu.astype(BF16)
    w_down_all = ffn_w_down.astype(BF16)
    for i in range(DEPTH):
        j = i // 2
        g_mix = norm_mix[i].reshape(1, D)
        if i % 2 == 0:
            nq = A_HEADS * A_HEAD_DIM
            nqk = nq + A_KV_HEADS * A_HEAD_DIM
            w_qkv = a_w_qkv[j]
            q, kcat, vt = _norm_qkv(h, g_mix, w_qkv[:, :nqk].astype(BF16),
                                    w_qkv[:, nqk:].T.astype(BF16))
            attn = _win_attn(q.reshape(B, S, nq), kcat, vt, pos_col3, pos_blk3, a_sink[j])
            attn = attn.reshape(T, nq)
            w_o = a_w_o[j].astype(BF16)
        else:
            w_in = b_w_in[j]
            n_lat = Q_LORA + KV_LORA
            w_kr = _head_block(jnp.zeros((D, 1, NOPE_DIM), F32),
                               w_in[:, n_lat:].reshape(D, 1, ROPE_DIM))
            w_in = jnp.concatenate([w_in[:, :n_lat], w_kr], axis=1).astype(BF16)
            w_uq = b_w_uq[j].reshape(Q_LORA, B_HEADS, NOPE_DIM + ROPE_DIM)
            w_uq = _head_block(w_uq[..., :NOPE_DIM], w_uq[..., NOPE_DIM:]).astype(BF16)
            w_ukv = b_w_ukv[j].reshape(KV_LORA, B_HEADS, NOPE_DIM + V_DIM)
            w_uk = _head_block(w_ukv[..., :NOPE_DIM],
                               jnp.zeros((KV_LORA, B_HEADS, ROPE_DIM), F32)).astype(BF16)
            w_uv = w_ukv[..., NOPE_DIM:].reshape(KV_LORA, B_HEADS * V_DIM).T.astype(BF16)
            qh, kh, v = _mla_prep(h.reshape(B, S, D), g_mix, w_in,
                                  b_g_q[j].reshape(1, Q_LORA), b_g_kv[j].reshape(1, KV_LORA),
                                  w_uq, w_uk, w_uv, ct, sg)
            attn = _mla_attn(qh, kh, v).reshape(T, B_HEADS * V_DIM)
            w_o = b_w_o[j].astype(BF16)
        h = _proj_ffn(h, attn, w_o, norm_ffn[i].reshape(1, D), w_gu_all, w_down_all, i,
                      gf, final_norm=(i == DEPTH - 1))
    return h.reshape(B, S, D)
```

```python
import functools
import math

import jax
import jax.numpy as jnp
from jax import lax
from jax.experimental import pallas as pl
from jax.experimental.pallas import tpu as pltpu

F32 = jnp.float32
BF16 = jnp.bfloat16

D_MODEL = 1024
DEPTH = 4
A_HEADS = 16
A_KV_HEADS = 4
A_GROUP = A_HEADS // A_KV_HEADS
A_HEAD_DIM = 64
WINDOW = 128
BLK = 128
B_HEADS = 16
Q_LORA = 384
KV_LORA = 256
NOPE_DIM = 64
ROPE_DIM = 32
V_DIM = 64
ROPE_THETA = 10000.0
D_FF = 2816
EPS = 1e-6

LOG2E = math.log2(math.e)
MASK_DIST = 1e36

LANES = 128
ONES_ROWS = 16
VMEM_LIMIT = 56 * 1024 * 1024

TM_PROJ = 512
TM_FFN = 1024
TF_FFN = 256
TM_PREP = 512
SUB_PREP = 256

ROPE_HALF = ROPE_DIM // 2
ROPE_PARTNER = LANES // 2
TQ_MLA = 256
KC_MLA = 256
PAIRS_MLA = 4

NT_DIMS = (((1,), (1,)), ((), ()))


def _rms(x, g):
    ms = jnp.mean(x * x, axis=-1, keepdims=True)
    return x * lax.rsqrt(ms + EPS) * g


def _params(n_axes):
    return pltpu.CompilerParams(dimension_semantics=("arbitrary",) * n_axes,
                                vmem_limit_bytes=VMEM_LIMIT)


def _layer_spec(stacked, layer, **kwargs):
    index = (layer,) + (0,) * (stacked.ndim - 1)
    return pl.BlockSpec((None,) + stacked.shape[1:], lambda *_: index, **kwargs)


def _norm_qkv_kernel(x_ref, g_ref, wqk_ref, wvt_ref, q_ref, k_ref, vt_ref):
    nq = A_HEADS * A_HEAD_DIM
    tm = x_ref.shape[0]
    xn = _rms(x_ref[...], g_ref[...]).astype(BF16)
    qk = jnp.dot(xn, wqk_ref[...], preferred_element_type=F32)
    q_ref[...] = (qk[:, :nq] * (A_HEAD_DIM ** -0.5 * LOG2E)).astype(BF16)
    low = lax.broadcasted_iota(jnp.int32, (1, LANES), 1) < A_HEAD_DIM
    for kh in range(A_KV_HEADS):
        pair = qk[:, nq + (kh // 2) * LANES:nq + (kh // 2 + 1) * LANES]
        swap = pltpu.roll(pair, A_HEAD_DIM, 1)
        own, other = (pair, swap) if kh % 2 == 0 else (swap, pair)
        k_ref[kh, 0] = jnp.where(low, own, 0.0).astype(BF16)
        k_ref[kh, 1] = jnp.where(low, 0.0, other).astype(BF16)
    vt = lax.dot_general(wvt_ref[...], xn, NT_DIMS, preferred_element_type=F32)
    ones = jnp.ones((ONES_ROWS, BLK), BF16)
    for kh in range(A_KV_HEADS):
        for tb in range(tm // BLK):
            vt_ref[kh, tb, :A_HEAD_DIM, :] = (
                vt[kh * A_HEAD_DIM:(kh + 1) * A_HEAD_DIM, tb * BLK:(tb + 1) * BLK].astype(BF16))
            vt_ref[kh, tb, A_HEAD_DIM:, :] = ones


def _norm_qkv(x, g_all, layer, wqk_all, wvt_all, mixer):
    T, D = x.shape
    nq = A_HEADS * A_HEAD_DIM
    tm = TM_PROJ
    return pl.pallas_call(
        _norm_qkv_kernel,
        grid=(T // tm,),
        in_specs=[pl.BlockSpec((tm, D), lambda i: (i, 0)),
                  _layer_spec(g_all, layer),
                  _layer_spec(wqk_all, mixer),
                  _layer_spec(wvt_all, mixer)],
        out_specs=[pl.BlockSpec((tm, nq), lambda i: (i, 0)),
                   pl.BlockSpec((A_KV_HEADS, 2, tm, LANES), lambda i: (0, 0, i, 0)),
                   pl.BlockSpec((A_KV_HEADS, tm // BLK, A_HEAD_DIM + ONES_ROWS, BLK),
                                lambda i: (0, i, 0, 0))],
        out_shape=[jax.ShapeDtypeStruct((T, nq), BF16),
                   jax.ShapeDtypeStruct((A_KV_HEADS, 2, T, LANES), BF16),
                   jax.ShapeDtypeStruct((A_KV_HEADS, T // BLK, A_HEAD_DIM + ONES_ROWS, BLK), BF16)],
        compiler_params=_params(1),
        name="norm_qkv",
    )(x, g_all, wqk_all, wvt_all)


def _alibi_slope(h):
    return float(2.0 ** (-8.0 * (h + 1) / A_HEADS))


def _win_attn_kernel(sink_ref, q_ref, k_ref, vt_ref, pc_ref, pr_ref, o_ref,
                     s0_sc, m0_sc, s1_sc, m1_sc, *, n_blocks):
    nkey = 3 * BLK
    r = lax.broadcasted_iota(jnp.int32, (nkey, BLK), 0)
    c = lax.broadcasted_iota(jnp.int32, (nkey, BLK), 1)
    rc = r - c

    def window(blk):
        if isinstance(blk, int):
            return min(max(blk - 1, 0), n_blocks - 3)
        return jnp.clip(blk - 1, 0, n_blocks - 3)

    def row_start(blk):
        return blk * BLK if isinstance(blk, int) else pl.multiple_of(blk * BLK, BLK)

    def masked_distance(blk):
        wb = window(blk)
        valid = jnp.abs(rc + (wb - blk) * BLK) <= WINDOW
        kpos = pc_ref[pl.ds(row_start(wb), nkey), :]
        qpos = pr_ref[pl.ds(blk, 1), :]
        return jnp.where(valid, jnp.abs(kpos - qpos).astype(F32), MASK_DIST)

    def scores(blk, kh, dm, s_sc, m_sc):
        r0 = row_start(blk)
        w0 = row_start(window(blk))
        col = kh * A_GROUP * A_HEAD_DIM
        qst = jnp.concatenate([q_ref[pl.ds(r0, BLK), col:col + LANES],
                               q_ref[pl.ds(r0, BLK), col + LANES:col + 2 * LANES]], axis=0)
        for part in range(2):
            st = lax.dot_general(k_ref[kh, part, pl.ds(w0, nkey), :], qst, NT_DIMS,
                                 preferred_element_type=F32)
            for half in range(2):
                h = kh * A_GROUP + 2 * half + part
                s = st[:, half * BLK:(half + 1) * BLK] - (_alibi_slope(h) * LOG2E) * dm
                s_sc[h] = s
                m = jnp.maximum(jnp.max(s, axis=0, keepdims=True), sink_ref[h] * LOG2E)
                m_sc[h] = jnp.broadcast_to(m, (8, BLK))

    def context(blk, kh, s_sc, m_sc):
        r0 = row_start(blk)
        wb = window(blk)
        col = kh * A_GROUP * A_HEAD_DIM
        vt = jnp.concatenate([vt_ref[kh, wb + t] for t in range(3)], axis=1)
        normed = [None] * A_GROUP
        for part in range(2):
            heads = [kh * A_GROUP + 2 * half + part for half in range(2)]
            pt = jnp.concatenate(
                [jnp.exp2(s_sc[h] - m_sc[h][0:1, :]).astype(BF16) for h in heads], axis=1)
            ot = jnp.dot(vt, pt, preferred_element_type=F32)
            for half, h in enumerate(heads):
                cols = slice(half * BLK, (half + 1) * BLK)
                l = ot[A_HEAD_DIM:A_HEAD_DIM + 1, cols] + jnp.exp2(
                    sink_ref[h] * LOG2E - m_sc[h][0:1, :])
                normed[2 * half + part] = ot[:A_HEAD_DIM, cols] / l
        for half in range(A_GROUP // 2):
            pair_t = jnp.concatenate([normed[2 * half], normed[2 * half + 1]], axis=0)
            o_ref[pl.ds(r0, BLK), col + half * LANES:col + (half + 1) * LANES] = (
                pair_t.T.astype(BF16))

    def stage(ctx, sc, s_old, m_old, s_new, m_new):
        dm = None if sc is None else masked_distance(sc)
        for kh in range(A_KV_HEADS):
            if ctx is not None:
                context(ctx, kh, s_old, m_old)
            if sc is not None:
                scores(sc, kh, dm, s_new, m_new)

    stage(None, 0, None, None, s0_sc, m0_sc)

    def body(j, carry):
        stage(2 * j, 2 * j + 1, s0_sc, m0_sc, s1_sc, m1_sc)
        stage(2 * j + 1, 2 * j + 2, s1_sc, m1_sc, s0_sc, m0_sc)
        return carry

    lax.fori_loop(0, n_blocks // 2 - 1, body, 0)
    stage(n_blocks - 2, n_blocks - 1, s0_sc, m0_sc, s1_sc, m1_sc)
    stage(n_blocks - 1, None, s1_sc, m1_sc, None, None)


def _win_attn(q, kcat, vt, pos_col, pos_blk, sink):
    B, S, nq = q.shape
    nb = S // BLK
    s_buf = pltpu.VMEM((A_HEADS, 3 * BLK, BLK), F32)
    m_buf = pltpu.VMEM((A_HEADS, 8, BLK), F32)
    return pl.pallas_call(
        functools.partial(_win_attn_kernel, n_blocks=nb),
        grid=(B,),
        in_specs=[pl.BlockSpec(memory_space=pltpu.SMEM),
                  pl.BlockSpec((None, S, nq), lambda b: (b, 0, 0)),
                  pl.BlockSpec((A_KV_HEADS, 2, S, LANES), lambda b: (0, 0, b, 0)),
                  pl.BlockSpec((A_KV_HEADS, nb, A_HEAD_DIM + ONES_ROWS, BLK), lambda b: (0, b, 0, 0)),
                  pl.BlockSpec((None, S, 1), lambda b: (b, 0, 0)),
                  pl.BlockSpec((None, nb, BLK), lambda b: (b, 0, 0))],
        out_specs=pl.BlockSpec((None, S, nq), lambda b: (b, 0, 0)),
        out_shape=jax.ShapeDtypeStruct((B, S, nq), BF16),
        scratch_shapes=[s_buf, m_buf, s_buf, m_buf],
        compiler_params=_params(1),
        name="win_attn",
    )(sink, q, kcat, vt, pos_col, pos_blk)


def _proj_ffn_kernel(h_ref, a_ref, wo_ref, g_ref, wgu_ref, wd_ref, gf_ref, out_ref, act_sc,
                     *, final_norm):
    h1 = h_ref[...] + jnp.dot(a_ref[...], wo_ref[...], preferred_element_type=F32)
    xn = _rms(h1, g_ref[...]).astype(BF16)
    for c in range(D_FF // TF_FFN):
        lo = c * TF_FFN
        gate = jnp.dot(xn, wgu_ref[:, lo:lo + TF_FFN], preferred_element_type=F32)
        up = jnp.dot(xn, wgu_ref[:, D_FF + lo:D_FF + lo + TF_FFN], preferred_element_type=F32)
        act_sc[:, lo:lo + TF_FFN] = (gate * jax.nn.sigmoid(gate) * up).astype(BF16)
    y = h1 + jnp.dot(act_sc[...], wd_ref[...], preferred_element_type=F32)
    if final_norm:
        y = _rms(y, gf_ref[...])
    out_ref[...] = y


def _proj_ffn(h, a, wo_all, mixer, g_all, wgu_all, wd_all, layer, gf, final_norm):
    T, D = h.shape
    tm = TM_FFN
    row = lambda i: (i, 0)
    resident = pl.Buffered(1)
    return pl.pallas_call(
        functools.partial(_proj_ffn_kernel, final_norm=final_norm),
        grid=(T // tm,),
        in_specs=[pl.BlockSpec((tm, D), row),
                  pl.BlockSpec((tm, a.shape[1]), row),
                  _layer_spec(wo_all, mixer, pipeline_mode=resident),
                  _layer_spec(g_all, layer),
                  _layer_spec(wgu_all, layer, pipeline_mode=resident),
                  _layer_spec(wd_all, layer, pipeline_mode=resident),
                  pl.BlockSpec((1, D), lambda i: (0, 0))],
        out_specs=pl.BlockSpec((tm, D), row),
        out_shape=jax.ShapeDtypeStruct((T, D), F32),
        scratch_shapes=[pltpu.VMEM((tm, D_FF), BF16)],
        compiler_params=_params(1),
        name="proj_ffn",
    )(h, a, wo_all, g_all, wgu_all, wd_all, gf)


def _rope_table_kernel(pos_ref, inv_ref, ct_ref, sg_ref):
    ang = pos_ref[...].astype(F32) * inv_ref[...]
    lane = lax.broadcasted_iota(jnp.int32, ang.shape, 1)
    first = lane < ROPE_HALF
    second = (lane >= ROPE_PARTNER) & (lane < ROPE_PARTNER + ROPE_HALF)
    sin = jnp.sin(ang)
    ct_ref[...] = jnp.where(first | second, jnp.cos(ang), 1.0)
    sg_ref[...] = jnp.where(first, -sin, jnp.where(second, sin, 0.0))


def _rope_tables(pos_col, inv_lane):
    T = pos_col.shape[0]
    tm = 2048
    tab = jax.ShapeDtypeStruct((T, LANES), F32)
    return pl.pallas_call(
        _rope_table_kernel,
        grid=(T // tm,),
        in_specs=[pl.BlockSpec((tm, 1), lambda i: (i, 0)),
                  pl.BlockSpec((1, LANES), lambda i: (0, 0))],
        out_specs=[pl.BlockSpec((tm, LANES), lambda i: (i, 0))] * 2,
        out_shape=[tab, tab],
        compiler_params=_params(1),
        name="rope_tables",
    )(pos_col, inv_lane)


def _mla_prep_kernel(x_ref, g_ref, win_ref, gq_ref, gkv_ref, wuq_ref, wuk_ref, wuvt_ref,
                     ct_ref, sg_ref, q_ref, k_ref, vt_ref):
    scale = (NOPE_DIM + ROPE_DIM) ** -0.5 * LOG2E
    for part in range(TM_PREP // SUB_PREP):
        rows = slice(part * SUB_PREP, (part + 1) * SUB_PREP)
        xn = _rms(x_ref[rows, :], g_ref[...]).astype(BF16)
        lat = jnp.dot(xn, win_ref[...], preferred_element_type=F32)
        cq = _rms(lat[:, :Q_LORA], gq_ref[...]).astype(BF16)
        ckv = _rms(lat[:, Q_LORA:Q_LORA + KV_LORA], gkv_ref[...]).astype(BF16)
        ct = ct_ref[rows, :]
        sg = sg_ref[rows, :]
        k_rope = lat[:, Q_LORA + KV_LORA:]
        k_rope = k_rope * ct + pltpu.roll(k_rope, ROPE_PARTNER, 1) * sg
        ctq = ct * scale
        sgq = sg * scale
        q = jnp.dot(cq, wuq_ref[...], preferred_element_type=F32)
        kn = jnp.dot(ckv, wuk_ref[...], preferred_element_type=F32)
        vt = lax.dot_general(wuvt_ref[...], ckv, NT_DIMS, preferred_element_type=F32)
        ones = jnp.ones((ONES_ROWS, SUB_PREP), BF16)
        for h in range(B_HEADS):
            sl = slice(h * LANES, (h + 1) * LANES)
            qh = q[:, sl]
            q_ref[h, rows, :] = (qh * ctq + pltpu.roll(qh, ROPE_PARTNER, 1) * sgq).astype(BF16)
            k_ref[h, rows, :] = (kn[:, sl] + k_rope).astype(BF16)
            vt_ref[h, :V_DIM, rows] = vt[h * V_DIM:(h + 1) * V_DIM, :].astype(BF16)
            vt_ref[h, V_DIM:, rows] = ones


def _mla_prep(x, g_all, layer, win, gq, gkv, wuq, wuk, wuvt, mixer, ct, sg):
    B, S, D = x.shape
    tm = TM_PREP
    row = lambda b, i: (b, i, 0)
    head = lambda b, i: (b, 0, i, 0)
    head_t = lambda b, i: (b, 0, 0, i)
    qk_shape = jax.ShapeDtypeStruct((B, B_HEADS, S, LANES), BF16)
    return pl.pallas_call(
        _mla_prep_kernel,
        grid=(B, S // tm),
        in_specs=[pl.BlockSpec((None, tm, D), row),
                  _layer_spec(g_all, layer),
                  _layer_spec(win, mixer),
                  _layer_spec(gq, mixer),
                  _layer_spec(gkv, mixer),
                  _layer_spec(wuq, mixer),
                  _layer_spec(wuk, mixer),
                  _layer_spec(wuvt, mixer),
                  pl.BlockSpec((None, tm, LANES), row),
                  pl.BlockSpec((None, tm, LANES), row)],
        out_specs=[pl.BlockSpec((None, B_HEADS, tm, LANES), head),
                   pl.BlockSpec((None, B_HEADS, tm, LANES), head),
                   pl.BlockSpec((None, B_HEADS, V_DIM + ONES_ROWS, tm), head_t)],
        out_shape=[qk_shape, qk_shape,
                   jax.ShapeDtypeStruct((B, B_HEADS, V_DIM + ONES_ROWS, S), BF16)],
        compiler_params=_params(2),
        name="mla_prep",
    )(x, g_all, win, gq, gkv, wuq, wuk, wuvt, ct, sg)


def _mla_attn_kernel(q_ref, k_ref, vt_ref, o_ref, s0_sc, m0_sc, s1_sc, m1_sc, acc_sc):
    S = q_ref.shape[1]
    n = S // TQ_MLA

    def row_start(blk):
        if isinstance(blk, int):
            return blk * TQ_MLA
        return pl.multiple_of(blk * TQ_MLA, TQ_MLA)

    def finish(item):
        hp, blk = item
        outs = [acc_sc[e][:V_DIM] / acc_sc[e][V_DIM:V_DIM + 1] for e in range(2)]
        o_ref[pl.ds(row_start(blk), TQ_MLA), hp * LANES:(hp + 1) * LANES] = (
            jnp.concatenate(outs, axis=0).T.astype(BF16))

    def stage(fin, ctx, sc, s_old, m_old, s_new, m_new):
        if fin is not None:
            finish(fin)
        for e in range(2):
            if sc is not None:
                head_sc = 2 * sc[0] + e
                q = q_ref[head_sc, pl.ds(row_start(sc[1]), TQ_MLA), :]
                m_run = None
            if ctx is not None:
                head_ctx = 2 * ctx[0] + e
                m = m_old[e][0:1, :]
                acc = None
            for c in range(S // KC_MLA):
                keys = slice(c * KC_MLA, (c + 1) * KC_MLA)
                if ctx is not None:
                    pt = jnp.exp2(s_old[e, keys, :] - m).astype(BF16)
                    part = jnp.dot(vt_ref[head_ctx, :, keys], pt, preferred_element_type=F32)
                    acc = part if acc is None else acc + part
                if sc is not None:
                    st = lax.dot_general(k_ref[head_sc, keys, :], q, NT_DIMS,
                                         preferred_element_type=F32)
                    s_new[e, keys, :] = st
                    m_chunk = jnp.max(st, axis=0, keepdims=True)
                    m_run = m_chunk if m_run is None else jnp.maximum(m_run, m_chunk)
            if sc is not None:
                m_new[e] = jnp.broadcast_to(m_run, (8, TQ_MLA))
            if ctx is not None:
                acc_sc[e] = acc

    n_items = PAIRS_MLA * n
    scratch = ((s0_sc, m0_sc), (s1_sc, m1_sc))

    def item(t):
        return (t // n, t % n) if 0 <= t < n_items else None

    def static_stage(k):
        stage(item(k - 2), item(k - 1), item(k), *scratch[(k - 1) % 2], *scratch[k % 2])

    for hp in range(PAIRS_MLA):
        static_stage(hp * n)
        static_stage(hp * n + 1)

        def body(j, carry, hp=hp):
            stage((hp, 2 * j), (hp, 2 * j + 1), (hp, 2 * j + 2), *scratch[1], *scratch[0])
            stage((hp, 2 * j + 1), (hp, 2 * j + 2), (hp, 2 * j + 3), *scratch[0], *scratch[1])
            return carry

        lax.fori_loop(0, n // 2 - 1, body, 0)
    static_stage(n_items)
    static_stage(n_items + 1)


def _mla_attn(q, k, v):
    B, H, S, _ = q.shape
    heads = 2 * PAIRS_MLA
    pair = lambda b, g: (b, g, 0, 0)
    s_buf = pltpu.VMEM((2, S, TQ_MLA), F32)
    m_buf = pltpu.VMEM((2, 8, TQ_MLA), F32)
    return pl.pallas_call(
        _mla_attn_kernel,
        grid=(B, H // heads),
        in_specs=[pl.BlockSpec((None, heads, S, LANES), pair),
                  pl.BlockSpec((None, heads, S, LANES), pair),
                  pl.BlockSpec((None, heads, V_DIM + ONES_ROWS, S), pair)],
        out_specs=pl.BlockSpec((None, S, PAIRS_MLA * LANES), lambda b, g: (b, 0, g)),
        out_shape=jax.ShapeDtypeStruct((B, S, H * V_DIM), BF16),
        scratch_shapes=[s_buf, m_buf, s_buf, m_buf,
                        pltpu.VMEM((2, V_DIM + ONES_ROWS, TQ_MLA), F32)],
        compiler_params=_params(2),
        name="mla_attn",
    )(q, k, v)


def _head_block(nope, rope):
    split = ROPE_PARTNER - ROPE_HALF
    pad = jnp.zeros(nope.shape[:-1] + (LANES - NOPE_DIM - ROPE_DIM,), nope.dtype)
    w = jnp.concatenate([rope[..., :ROPE_HALF], nope[..., :split],
                         rope[..., ROPE_HALF:], nope[..., split:], pad], axis=-1)
    return w.reshape(w.shape[:-2] + (w.shape[-2] * LANES,))


def kernel(x, positions, norm_mix, norm_ffn, a_w_qkv, a_sink, a_w_o, b_w_in, b_g_q, b_g_kv,
           b_w_uq, b_w_ukv, b_w_o, ffn_w_gu, ffn_w_down, final_norm):
    B, S, D = x.shape
    T = B * S
    n_a = a_w_qkv.shape[0]
    n_b = b_w_in.shape[0]
    h = x.reshape(T, D)
    pos_col3 = positions.reshape(B, S, 1)
    pos_blk3 = positions.reshape(B, S // BLK, BLK)

    inv_freq = ROPE_THETA ** (-jnp.arange(ROPE_HALF, dtype=F32) * 2.0 / ROPE_DIM)
    inv_lane = jnp.zeros((1, LANES), F32)
    inv_lane = inv_lane.at[0, :ROPE_HALF].set(inv_freq)
    inv_lane = inv_lane.at[0, ROPE_PARTNER:ROPE_PARTNER + ROPE_HALF].set(inv_freq)
    ct, sg = _rope_tables(positions.reshape(T, 1), inv_lane)
    ct, sg = (t.reshape(B, S, LANES) for t in (ct, sg))

    g_mix = norm_mix.reshape(DEPTH, 1, D)
    g_ffn = norm_ffn.reshape(DEPTH, 1, D)
    gf = final_norm.reshape(1, D)
    w_gu = ffn_w_gu.astype(BF16)
    w_down = ffn_w_down.astype(BF16)

    nq = A_HEADS * A_HEAD_DIM
    nqk = nq + A_KV_HEADS * A_HEAD_DIM
    a_wqk = a_w_qkv[:, :, :nqk].astype(BF16)
    a_wvt = jnp.swapaxes(a_w_qkv[:, :, nqk:], 1, 2).astype(BF16)
    a_wo = a_w_o.astype(BF16)

    n_lat = Q_LORA + KV_LORA
    w_kr = _head_block(jnp.zeros((n_b, D, 1, NOPE_DIM), F32),
                       b_w_in[:, :, n_lat:].reshape(n_b, D, 1, ROPE_DIM))
    b_win = jnp.concatenate([b_w_in[:, :, :n_lat], w_kr], axis=-1).astype(BF16)
    w_uq = b_w_uq.reshape(n_b, Q_LORA, B_HEADS, NOPE_DIM + ROPE_DIM)
    b_wuq = _head_block(w_uq[..., :NOPE_DIM], w_uq[..., NOPE_DIM:]).astype(BF16)
    w_ukv = b_w_ukv.reshape(n_b, KV_LORA, B_HEADS, NOPE_DIM + V_DIM)
    b_wuk = _head_block(w_ukv[..., :NOPE_DIM],
                        jnp.zeros((n_b, KV_LORA, B_HEADS, ROPE_DIM), F32)).astype(BF16)
    b_wuvt = jnp.swapaxes(w_ukv[..., NOPE_DIM:].reshape(n_b, KV_LORA, B_HEADS * V_DIM),
                          1, 2).astype(BF16)
    b_gq = b_g_q.reshape(n_b, 1, Q_LORA)
    b_gkv = b_g_kv.reshape(n_b, 1, KV_LORA)
    b_wo = b_w_o.astype(BF16)

    for i in range(DEPTH):
        j = i // 2
        if i % 2 == 0:
            q, kcat, vt = _norm_qkv(h, g_mix, i, a_wqk, a_wvt, j)
            attn = _win_attn(q.reshape(B, S, nq), kcat, vt, pos_col3, pos_blk3, a_sink[j])
            attn = attn.reshape(T, nq)
            w_o = a_wo
        else:
            qh, kh, v = _mla_prep(h.reshape(B, S, D), g_mix, i, b_win, b_gq, b_gkv,
                                  b_wuq, b_wuk, b_wuvt, j, ct, sg)
            attn = _mla_attn(qh, kh, v).reshape(T, B_HEADS * V_DIM)
            w_o = b_wo
        h = _proj_ffn(h, attn, w_o, j, g_ffn, w_gu, w_down, i, gf,
                      final_norm=(i == DEPTH - 1))
    return h.reshape(B, S, D)
```

```python
import functools
import math

import jax
import jax.numpy as jnp
from jax import lax
from jax.experimental import pallas as pl
from jax.experimental.pallas import tpu as pltpu

F32 = jnp.float32
BF16 = jnp.bfloat16

D_MODEL = 1024
DEPTH = 4
A_HEADS = 16
A_KV_HEADS = 4
A_GROUP = A_HEADS // A_KV_HEADS
A_HEAD_DIM = 64
WINDOW = 128
BLK = 128
B_HEADS = 16
Q_LORA = 384
KV_LORA = 256
NOPE_DIM = 64
ROPE_DIM = 32
V_DIM = 64
ROPE_THETA = 10000.0
D_FF = 2816
EPS = 1e-6

LOG2E = math.log2(math.e)
MASK_DIST = 1e36

LANES = 128
ONES_ROWS = 64
VMEM_LIMIT = 56 * 1024 * 1024

TM_PROJ = 512
TM_FFN = 1024
TF_FFN = 256
TM_PREP = 512
SUB_PREP = 256

ROPE_HALF = ROPE_DIM // 2
ROPE_PARTNER = LANES // 2
TQ_MLA = 256
KC_MLA = 256
PAIRS_MLA = 4

NT_DIMS = (((1,), (1,)), ((), ()))


def _rms(x, g):
    ms = jnp.mean(x * x, axis=-1, keepdims=True)
    return x * lax.rsqrt(ms + EPS) * g


def _params(n_axes):
    return pltpu.CompilerParams(dimension_semantics=("arbitrary",) * n_axes,
                                vmem_limit_bytes=VMEM_LIMIT)


def _layer_spec(stacked, layer, **kwargs):
    index = (layer,) + (0,) * (stacked.ndim - 1)
    return pl.BlockSpec((None,) + stacked.shape[1:], lambda *_: index, **kwargs)


def _norm_qkv_kernel(x_ref, g_ref, wqk_ref, wvt_ref, q_ref, k_ref, vt_ref):
    nq = A_HEADS * A_HEAD_DIM
    tm = x_ref.shape[0]
    xn = _rms(x_ref[...], g_ref[...]).astype(BF16)
    qk = jnp.dot(xn, wqk_ref[...], preferred_element_type=F32)
    q_ref[...] = (qk[:, :nq] * (A_HEAD_DIM ** -0.5 * LOG2E)).astype(BF16)
    low = lax.broadcasted_iota(jnp.int32, (1, LANES), 1) < A_HEAD_DIM
    for kh in range(A_KV_HEADS):
        pair = qk[:, nq + (kh // 2) * LANES:nq + (kh // 2 + 1) * LANES]
        swap = pltpu.roll(pair, A_HEAD_DIM, 1)
        own, other = (pair, swap) if kh % 2 == 0 else (swap, pair)
        k_ref[kh, 0] = jnp.where(low, own, 0.0).astype(BF16)
        k_ref[kh, 1] = jnp.where(low, 0.0, other).astype(BF16)
    vt = lax.dot_general(wvt_ref[...], xn, NT_DIMS, preferred_element_type=F32)
    ones = jnp.ones((ONES_ROWS, BLK), BF16)
    for kh in range(A_KV_HEADS):
        for tb in range(tm // BLK):
            vt_ref[kh, tb, :A_HEAD_DIM, :] = (
                vt[kh * A_HEAD_DIM:(kh + 1) * A_HEAD_DIM, tb * BLK:(tb + 1) * BLK].astype(BF16))
            vt_ref[kh, tb, A_HEAD_DIM:, :] = ones


def _norm_qkv(x, g_all, layer, wqk_all, wvt_all, mixer):
    T, D = x.shape
    nq = A_HEADS * A_HEAD_DIM
    tm = TM_PROJ
    return pl.pallas_call(
        _norm_qkv_kernel,
        grid=(T // tm,),
        in_specs=[pl.BlockSpec((tm, D), lambda i: (i, 0)),
                  _layer_spec(g_all, layer),
                  _layer_spec(wqk_all, mixer),
                  _layer_spec(wvt_all, mixer)],
        out_specs=[pl.BlockSpec((tm, nq), lambda i: (i, 0)),
                   pl.BlockSpec((A_KV_HEADS, 2, tm, LANES), lambda i: (0, 0, i, 0)),
                   pl.BlockSpec((A_KV_HEADS, tm // BLK, A_HEAD_DIM + ONES_ROWS, BLK),
                                lambda i: (0, i, 0, 0))],
        out_shape=[jax.ShapeDtypeStruct((T, nq), BF16),
                   jax.ShapeDtypeStruct((A_KV_HEADS, 2, T, LANES), BF16),
                   jax.ShapeDtypeStruct((A_KV_HEADS, T // BLK, A_HEAD_DIM + ONES_ROWS, BLK), BF16)],
        compiler_params=_params(1),
        name="norm_qkv",
    )(x, g_all, wqk_all, wvt_all)


def _alibi_slope(h):
    return float(2.0 ** (-8.0 * (h + 1) / A_HEADS))


def _win_attn_kernel(sink_ref, q_ref, k_ref, vt_ref, pc_ref, pr_ref, o_ref,
                     s0_sc, m0_sc, s1_sc, m1_sc, *, n_blocks):
    nkey = 3 * BLK
    r = lax.broadcasted_iota(jnp.int32, (nkey, BLK), 0)
    c = lax.broadcasted_iota(jnp.int32, (nkey, BLK), 1)
    rc = r - c

    def window(blk):
        if isinstance(blk, int):
            return min(max(blk - 1, 0), n_blocks - 3)
        return jnp.clip(blk - 1, 0, n_blocks - 3)

    def row_start(blk):
        return blk * BLK if isinstance(blk, int) else pl.multiple_of(blk * BLK, BLK)

    def masked_distance(blk):
        wb = window(blk)
        valid = jnp.abs(rc + (wb - blk) * BLK) <= WINDOW
        kpos = pc_ref[pl.ds(row_start(wb), nkey), :]
        qpos = pr_ref[pl.ds(blk, 1), :]
        return jnp.where(valid, jnp.abs(kpos - qpos).astype(F32), MASK_DIST)

    def scores(blk, kh, dm, s_sc, m_sc):
        r0 = row_start(blk)
        w0 = row_start(window(blk))
        col = kh * A_GROUP * A_HEAD_DIM
        qst = jnp.concatenate([q_ref[pl.ds(r0, BLK), col:col + LANES],
                               q_ref[pl.ds(r0, BLK), col + LANES:col + 2 * LANES]], axis=0)
        for part in range(2):
            st = lax.dot_general(k_ref[kh, part, pl.ds(w0, nkey), :], qst, NT_DIMS,
                                 preferred_element_type=F32)
            for half in range(2):
                h = kh * A_GROUP + 2 * half + part
                s = st[:, half * BLK:(half + 1) * BLK] - (_alibi_slope(h) * LOG2E) * dm
                s_sc[h] = s
                m = jnp.maximum(jnp.max(s, axis=0, keepdims=True), sink_ref[h] * LOG2E)
                m_sc[h] = jnp.broadcast_to(m, (8, BLK))

    def context(blk, kh, s_sc, m_sc):
        r0 = row_start(blk)
        wb = window(blk)
        col = kh * A_GROUP * A_HEAD_DIM
        vt = jnp.concatenate([vt_ref[kh, wb + t] for t in range(3)], axis=1)
        normed = [None] * A_GROUP
        for part in range(2):
            heads = [kh * A_GROUP + 2 * half + part for half in range(2)]
            pt = jnp.concatenate(
                [jnp.exp2(s_sc[h] - m_sc[h][0:1, :]).astype(BF16) for h in heads], axis=1)
            ot = jnp.dot(vt, pt, preferred_element_type=F32)
            for half, h in enumerate(heads):
                cols = slice(half * BLK, (half + 1) * BLK)
                l = ot[A_HEAD_DIM:A_HEAD_DIM + 1, cols] + jnp.exp2(
                    sink_ref[h] * LOG2E - m_sc[h][0:1, :])
                normed[2 * half + part] = ot[:A_HEAD_DIM, cols] / l
        for half in range(A_GROUP // 2):
            pair_t = jnp.concatenate([normed[2 * half], normed[2 * half + 1]], axis=0)
            o_ref[pl.ds(r0, BLK), col + half * LANES:col + (half + 1) * LANES] = (
                pair_t.T.astype(BF16))

    def stage(ctx, sc, s_old, m_old, s_new, m_new):
        dm = None if sc is None else masked_distance(sc)
        for kh in range(A_KV_HEADS):
            if ctx is not None:
                context(ctx, kh, s_old, m_old)
            if sc is not None:
                scores(sc, kh, dm, s_new, m_new)

    stage(None, 0, None, None, s0_sc, m0_sc)

    def body(j, carry):
        stage(2 * j, 2 * j + 1, s0_sc, m0_sc, s1_sc, m1_sc)
        stage(2 * j + 1, 2 * j + 2, s1_sc, m1_sc, s0_sc, m0_sc)
        return carry

    lax.fori_loop(0, n_blocks // 2 - 1, body, 0)
    stage(n_blocks - 2, n_blocks - 1, s0_sc, m0_sc, s1_sc, m1_sc)
    stage(n_blocks - 1, None, s1_sc, m1_sc, None, None)


def _win_attn(q, kcat, vt, pos_col, pos_blk, sink):
    B, S, nq = q.shape
    nb = S // BLK
    s_buf = pltpu.VMEM((A_HEADS, 3 * BLK, BLK), F32)
    m_buf = pltpu.VMEM((A_HEADS, 8, BLK), F32)
    return pl.pallas_call(
        functools.partial(_win_attn_kernel, n_blocks=nb),
        grid=(B,),
        in_specs=[pl.BlockSpec(memory_space=pltpu.SMEM),
                  pl.BlockSpec((None, S, nq), lambda b: (b, 0, 0)),
                  pl.BlockSpec((A_KV_HEADS, 2, S, LANES), lambda b: (0, 0, b, 0)),
                  pl.BlockSpec((A_KV_HEADS, nb, A_HEAD_DIM + ONES_ROWS, BLK), lambda b: (0, b, 0, 0)),
                  pl.BlockSpec((None, S, 1), lambda b: (b, 0, 0)),
                  pl.BlockSpec((None, nb, BLK), lambda b: (b, 0, 0))],
        out_specs=pl.BlockSpec((None, S, nq), lambda b: (b, 0, 0)),
        out_shape=jax.ShapeDtypeStruct((B, S, nq), BF16),
        scratch_shapes=[s_buf, m_buf, s_buf, m_buf],
        compiler_params=_params(1),
        name="win_attn",
    )(sink, q, kcat, vt, pos_col, pos_blk)


def _proj_ffn_kernel(h_ref, a_ref, wo_ref, g_ref, wgu_ref, wd_ref, gf_ref, out_ref, act_sc,
                     *, final_norm):
    h1 = h_ref[...] + jnp.dot(a_ref[...], wo_ref[...], preferred_element_type=F32)
    xn = _rms(h1, g_ref[...]).astype(BF16)
    for c in range(D_FF // TF_FFN):
        lo = c * TF_FFN
        gate = jnp.dot(xn, wgu_ref[:, lo:lo + TF_FFN], preferred_element_type=F32)
        up = jnp.dot(xn, wgu_ref[:, D_FF + lo:D_FF + lo + TF_FFN], preferred_element_type=F32)
        act_sc[:, lo:lo + TF_FFN] = (gate * jax.nn.sigmoid(gate) * up).astype(BF16)
    y = h1 + jnp.dot(act_sc[...], wd_ref[...], preferred_element_type=F32)
    if final_norm:
        y = _rms(y, gf_ref[...])
    out_ref[...] = y


def _proj_ffn(h, a, wo_all, mixer, g_all, wgu_all, wd_all, layer, gf, final_norm):
    T, D = h.shape
    tm = TM_FFN
    row = lambda i: (i, 0)
    resident = pl.Buffered(1)
    return pl.pallas_call(
        functools.partial(_proj_ffn_kernel, final_norm=final_norm),
        grid=(T // tm,),
        in_specs=[pl.BlockSpec((tm, D), row),
                  pl.BlockSpec((tm, a.shape[1]), row),
                  _layer_spec(wo_all, mixer, pipeline_mode=resident),
                  _layer_spec(g_all, layer),
                  _layer_spec(wgu_all, layer, pipeline_mode=resident),
                  _layer_spec(wd_all, layer, pipeline_mode=resident),
                  pl.BlockSpec((1, D), lambda i: (0, 0))],
        out_specs=pl.BlockSpec((tm, D), row),
        out_shape=jax.ShapeDtypeStruct((T, D), F32),
        scratch_shapes=[pltpu.VMEM((tm, D_FF), BF16)],
        compiler_params=_params(1),
        name="proj_ffn",
    )(h, a, wo_all, g_all, wgu_all, wd_all, gf)


TOK_PER_ROW = LANES // ROPE_HALF


def _rope_table_kernel(pos_ref, inv_ref, ct_ref, sg_ref):
    ang = pos_ref[...].astype(F32) * inv_ref[...]
    cos = jnp.cos(ang)
    sin = jnp.sin(ang)
    rows = ang.shape[0]
    lane = lax.broadcasted_iota(jnp.int32, ang.shape, 1)
    first = lane < ROPE_HALF
    second = (lane >= ROPE_PARTNER) & (lane < ROPE_PARTNER + ROPE_HALF)
    for g in range(TOK_PER_ROW):
        to_first = (LANES - g * ROPE_HALF) % LANES
        to_second = (to_first + ROPE_PARTNER) % LANES
        c1, c2 = (cos if s == 0 else pltpu.roll(cos, s, 1) for s in (to_first, to_second))
        s1, s2 = (sin if s == 0 else pltpu.roll(sin, s, 1) for s in (to_first, to_second))
        tokens = pl.ds(g, rows, stride=TOK_PER_ROW)
        ct_ref[tokens, :] = jnp.where(first, c1, jnp.where(second, c2, 1.0))
        sg_ref[tokens, :] = jnp.where(first, -s1, jnp.where(second, s2, 0.0))


def _rope_tables(pos_dense, inv_dense):
    T = pos_dense.shape[0] * TOK_PER_ROW
    tm = 2048
    tab = jax.ShapeDtypeStruct((T, LANES), F32)
    return pl.pallas_call(
        _rope_table_kernel,
        grid=(T // tm,),
        in_specs=[pl.BlockSpec((tm // TOK_PER_ROW, LANES), lambda i: (i, 0)),
                  pl.BlockSpec((1, LANES), lambda i: (0, 0))],
        out_specs=[pl.BlockSpec((tm, LANES), lambda i: (i, 0))] * 2,
        out_shape=[tab, tab],
        compiler_params=_params(1),
        name="rope_tables",
    )(pos_dense, inv_dense)


def _mla_prep_kernel(x_ref, g_ref, win_ref, gq_ref, gkv_ref, wuq_ref, wuk_ref, wuvt_ref,
                     ct_ref, sg_ref, q_ref, k_ref, vt_ref):
    scale = (NOPE_DIM + ROPE_DIM) ** -0.5 * LOG2E
    for part in range(TM_PREP // SUB_PREP):
        rows = slice(part * SUB_PREP, (part + 1) * SUB_PREP)
        xn = _rms(x_ref[rows, :], g_ref[...]).astype(BF16)
        lat = jnp.dot(xn, win_ref[...], preferred_element_type=F32)
        cq = _rms(lat[:, :Q_LORA], gq_ref[...]).astype(BF16)
        ckv = _rms(lat[:, Q_LORA:Q_LORA + KV_LORA], gkv_ref[...]).astype(BF16)
        ct = ct_ref[rows, :]
        sg = sg_ref[rows, :]
        k_rope = lat[:, Q_LORA + KV_LORA:]
        k_rope = k_rope * ct + pltpu.roll(k_rope, ROPE_PARTNER, 1) * sg
        ctq = ct * scale
        sgq = sg * scale
        q = jnp.dot(cq, wuq_ref[...], preferred_element_type=F32)
        kn = jnp.dot(ckv, wuk_ref[...], preferred_element_type=F32)
        vt = lax.dot_general(wuvt_ref[...], ckv, NT_DIMS, preferred_element_type=F32)
        ones = jnp.ones((ONES_ROWS, SUB_PREP), BF16)
        for h in range(B_HEADS):
            sl = slice(h * LANES, (h + 1) * LANES)
            qh = q[:, sl]
            q_ref[h, rows, :] = (qh * ctq + pltpu.roll(qh, ROPE_PARTNER, 1) * sgq).astype(BF16)
            k_ref[h, rows, :] = (kn[:, sl] + k_rope).astype(BF16)
            vt_ref[h, :V_DIM, rows] = vt[h * V_DIM:(h + 1) * V_DIM, :].astype(BF16)
            vt_ref[h, V_DIM:, rows] = ones


def _mla_prep(x, g_all, layer, win, gq, gkv, wuq, wuk, wuvt, mixer, ct, sg):
    B, S, D = x.shape
    tm = TM_PREP
    row = lambda b, i: (b, i, 0)
    head = lambda b, i: (b, 0, i, 0)
    head_t = lambda b, i: (b, 0, 0, i)
    qk_shape = jax.ShapeDtypeStruct((B, B_HEADS, S, LANES), BF16)
    return pl.pallas_call(
        _mla_prep_kernel,
        grid=(B, S // tm),
        in_specs=[pl.BlockSpec((None, tm, D), row),
                  _layer_spec(g_all, layer),
                  _layer_spec(win, mixer),
                  _layer_spec(gq, mixer),
                  _layer_spec(gkv, mixer),
                  _layer_spec(wuq, mixer),
                  _layer_spec(wuk, mixer),
                  _layer_spec(wuvt, mixer),
                  pl.BlockSpec((None, tm, LANES), row),
                  pl.BlockSpec((None, tm, LANES), row)],
        out_specs=[pl.BlockSpec((None, B_HEADS, tm, LANES), head),
                   pl.BlockSpec((None, B_HEADS, tm, LANES), head),
                   pl.BlockSpec((None, B_HEADS, V_DIM + ONES_ROWS, tm), head_t)],
        out_shape=[qk_shape, qk_shape,
                   jax.ShapeDtypeStruct((B, B_HEADS, V_DIM + ONES_ROWS, S), BF16)],
        compiler_params=_params(2),
        name="mla_prep",
    )(x, g_all, win, gq, gkv, wuq, wuk, wuvt, ct, sg)


def _mla_attn_kernel(q_ref, k_ref, vt_ref, o_ref, s0_sc, m0_sc, s1_sc, m1_sc, acc_sc):
    S = q_ref.shape[1]
    n = S // TQ_MLA

    def row_start(blk):
        if isinstance(blk, int):
            return blk * TQ_MLA
        return pl.multiple_of(blk * TQ_MLA, TQ_MLA)

    def finish(item):
        hp, blk = item
        outs = [acc_sc[e][:V_DIM] / acc_sc[e][V_DIM:V_DIM + 1] for e in range(2)]
        o_ref[pl.ds(row_start(blk), TQ_MLA), hp * LANES:(hp + 1) * LANES] = (
            jnp.concatenate(outs, axis=0).T.astype(BF16))

    def stage(fin, ctx, sc, s_old, m_old, s_new, m_new):
        if fin is not None:
            finish(fin)
        for e in range(2):
            if sc is not None:
                head_sc = 2 * sc[0] + e
                q = q_ref[head_sc, pl.ds(row_start(sc[1]), TQ_MLA), :]
                m_run = None
            if ctx is not None:
                head_ctx = 2 * ctx[0] + e
                m = m_old[e][0:1, :]
                acc = None
            for c in range(S // KC_MLA):
                keys = slice(c * KC_MLA, (c + 1) * KC_MLA)
                if ctx is not None:
                    pt = jnp.exp2(s_old[e, keys, :] - m).astype(BF16)
                    part = jnp.dot(vt_ref[head_ctx, :, keys], pt, preferred_element_type=F32)
                    acc = part if acc is None else acc + part
                if sc is not None:
                    st = lax.dot_general(k_ref[head_sc, keys, :], q, NT_DIMS,
                                         preferred_element_type=F32)
                    s_new[e, keys, :] = st
                    m_chunk = jnp.max(st, axis=0, keepdims=True)
                    m_run = m_chunk if m_run is None else jnp.maximum(m_run, m_chunk)
            if sc is not None:
                m_new[e] = jnp.broadcast_to(m_run, (8, TQ_MLA))
            if ctx is not None:
                acc_sc[e] = acc

    n_items = PAIRS_MLA * n
    scratch = ((s0_sc, m0_sc), (s1_sc, m1_sc))

    def item(t):
        return (t // n, t % n) if 0 <= t < n_items else None

    def static_stage(k):
        stage(item(k - 2), item(k - 1), item(k), *scratch[(k - 1) % 2], *scratch[k % 2])

    for hp in range(PAIRS_MLA):
        static_stage(hp * n)
        static_stage(hp * n + 1)

        def body(j, carry, hp=hp):
            stage((hp, 2 * j), (hp, 2 * j + 1), (hp, 2 * j + 2), *scratch[1], *scratch[0])
            stage((hp, 2 * j + 1), (hp, 2 * j + 2), (hp, 2 * j + 3), *scratch[0], *scratch[1])
            return carry

        lax.fori_loop(0, n // 2 - 1, body, 0)
    static_stage(n_items)
    static_stage(n_items + 1)


def _mla_attn(q, k, v):
    B, H, S, _ = q.shape
    heads = 2 * PAIRS_MLA
    pair = lambda b, g: (b, g, 0, 0)
    s_buf = pltpu.VMEM((2, S, TQ_MLA), F32)
    m_buf = pltpu.VMEM((2, 8, TQ_MLA), F32)
    return pl.pallas_call(
        _mla_attn_kernel,
        grid=(B, H // heads),
        in_specs=[pl.BlockSpec((None, heads, S, LANES), pair),
                  pl.BlockSpec((None, heads, S, LANES), pair),
                  pl.BlockSpec((None, heads, V_DIM + ONES_ROWS, S), pair)],
        out_specs=pl.BlockSpec((None, S, PAIRS_MLA * LANES), lambda b, g: (b, 0, g)),
        out_shape=jax.ShapeDtypeStruct((B, S, H * V_DIM), BF16),
        scratch_shapes=[s_buf, m_buf, s_buf, m_buf,
                        pltpu.VMEM((2, V_DIM + ONES_ROWS, TQ_MLA), F32)],
        compiler_params=_params(2),
        name="mla_attn",
    )(q, k, v)


def _head_block(nope, rope):
    split = ROPE_PARTNER - ROPE_HALF
    pad = jnp.zeros(nope.shape[:-1] + (LANES - NOPE_DIM - ROPE_DIM,), nope.dtype)
    w = jnp.concatenate([rope[..., :ROPE_HALF], nope[..., :split],
                         rope[..., ROPE_HALF:], nope[..., split:], pad], axis=-1)
    return w.reshape(w.shape[:-2] + (w.shape[-2] * LANES,))


def kernel(x, positions, norm_mix, norm_ffn, a_w_qkv, a_sink, a_w_o, b_w_in, b_g_q, b_g_kv,
           b_w_uq, b_w_ukv, b_w_o, ffn_w_gu, ffn_w_down, final_norm):
    B, S, D = x.shape
    T = B * S
    n_a = a_w_qkv.shape[0]
    n_b = b_w_in.shape[0]
    h = x.reshape(T, D)
    pos_col3 = positions.reshape(B, S, 1)
    pos_blk3 = positions.reshape(B, S // BLK, BLK)

    inv_freq = ROPE_THETA ** (-jnp.arange(ROPE_HALF, dtype=F32) * 2.0 / ROPE_DIM)
    pos_dense = jnp.repeat(positions.reshape(T // TOK_PER_ROW, TOK_PER_ROW), ROPE_HALF, axis=1)
    ct, sg = _rope_tables(pos_dense, jnp.tile(inv_freq, TOK_PER_ROW).reshape(1, LANES))
    ct, sg = (t.reshape(B, S, LANES) for t in (ct, sg))

    g_mix = norm_mix.reshape(DEPTH, 1, D)
    g_ffn = norm_ffn.reshape(DEPTH, 1, D)
    gf = final_norm.reshape(1, D)
    w_gu = ffn_w_gu.astype(BF16)
    w_down = ffn_w_down.astype(BF16)

    nq = A_HEADS * A_HEAD_DIM
    nqk = nq + A_KV_HEADS * A_HEAD_DIM
    a_wqk = a_w_qkv[:, :, :nqk].astype(BF16)
    a_wvt = jnp.swapaxes(a_w_qkv[:, :, nqk:], 1, 2).astype(BF16)
    a_wo = a_w_o.astype(BF16)

    n_lat = Q_LORA + KV_LORA
    w_kr = _head_block(jnp.zeros((n_b, D, 1, NOPE_DIM), F32),
                       b_w_in[:, :, n_lat:].reshape(n_b, D, 1, ROPE_DIM))
    b_win = jnp.concatenate([b_w_in[:, :, :n_lat], w_kr], axis=-1).astype(BF16)
    w_uq = b_w_uq.reshape(n_b, Q_LORA, B_HEADS, NOPE_DIM + ROPE_DIM)
    b_wuq = _head_block(w_uq[..., :NOPE_DIM], w_uq[..., NOPE_DIM:]).astype(BF16)
    w_ukv = b_w_ukv.reshape(n_b, KV_LORA, B_HEADS, NOPE_DIM + V_DIM)
    b_wuk = _head_block(w_ukv[..., :NOPE_DIM],
                        jnp.zeros((n_b, KV_LORA, B_HEADS, ROPE_DIM), F32)).astype(BF16)
    b_wuvt = jnp.swapaxes(w_ukv[..., NOPE_DIM:].reshape(n_b, KV_LORA, B_HEADS * V_DIM),
                          1, 2).astype(BF16)
    b_gq = b_g_q.reshape(n_b, 1, Q_LORA)
    b_gkv = b_g_kv.reshape(n_b, 1, KV_LORA)
    b_wo = b_w_o.astype(BF16)

    for i in range(DEPTH):
        j = i // 2
        if i % 2 == 0:
            q, kcat, vt = _norm_qkv(h, g_mix, i, a_wqk, a_wvt, j)
            attn = _win_attn(q.reshape(B, S, nq), kcat, vt, pos_col3, pos_blk3, a_sink[j])
            attn = attn.reshape(T, nq)
            w_o = a_wo
        else:
            qh, kh, v = _mla_prep(h.reshape(B, S, D), g_mix, i, b_win, b_gq, b_gkv,
                                  b_wuq, b_wuk, b_wuvt, j, ct, sg)
            attn = _mla_attn(qh, kh, v).reshape(T, B_HEADS * V_DIM)
            w_o = b_wo
        h = _proj_ffn(h, attn, w_o, j, g_ffn, w_gu, w_down, i, gf,
                      final_norm=(i == DEPTH - 1))
    return h.reshape(B, S, D)
```

```python
import functools
import math

import jax
import jax.numpy as jnp
from jax import lax
from jax.experimental import pallas as pl
from jax.experimental.pallas import tpu as pltpu

F32 = jnp.float32
BF16 = jnp.bfloat16

D_MODEL = 1024
DEPTH = 4
A_HEADS = 16
A_KV_HEADS = 4
A_GROUP = A_HEADS // A_KV_HEADS
A_HEAD_DIM = 64
WINDOW = 128
BLK = 128
B_HEADS = 16
Q_LORA = 384
KV_LORA = 256
NOPE_DIM = 64
ROPE_DIM = 32
V_DIM = 64
ROPE_THETA = 10000.0
D_FF = 2816
EPS = 1e-6

LOG2E = math.log2(math.e)
MASK_DIST = 1e36

LANES = 128
ONES_ROWS = 64
VMEM_LIMIT = 56 * 1024 * 1024

TM_PROJ = 512
TM_FFN = 1024
TF_FFN = 256
TM_PREP = 512
SUB_PREP = 256

ROPE_HALF = ROPE_DIM // 2
ROPE_PARTNER = LANES // 2
TQ_MLA = 256
KC_MLA = 256
PAIRS_MLA = 4

NT_DIMS = (((1,), (1,)), ((), ()))


def _rms(x, g):
    ms = jnp.mean(x * x, axis=-1, keepdims=True)
    return x * lax.rsqrt(ms + EPS) * g


def _params(n_axes):
    return pltpu.CompilerParams(dimension_semantics=("arbitrary",) * n_axes,
                                vmem_limit_bytes=VMEM_LIMIT)


def _layer_spec(stacked, layer, **kwargs):
    index = (layer,) + (0,) * (stacked.ndim - 1)
    return pl.BlockSpec((None,) + stacked.shape[1:], lambda *_: index, **kwargs)


def _ffn_weight_slabs(wgu_all, wd_all, layer, n_steps, step_of):
    specs_in, specs_out, shapes = [], [], []
    for w_all in (wgu_all, wd_all):
        rows, cols = w_all.shape[1:]
        slab = rows // n_steps
        specs_in.append(pl.BlockSpec((None, slab, cols),
                                     lambda *idx: (layer, step_of(*idx), 0)))
        specs_out.append(pl.BlockSpec((slab, cols), lambda *idx: (step_of(*idx), 0)))
        shapes.append(jax.ShapeDtypeStruct((rows, cols), BF16))
    return specs_in, specs_out, shapes


def _cast_ffn_weight_slab(wgu_ref, wd_ref, wgu_out_ref, wd_out_ref):
    wgu_out_ref[...] = wgu_ref[...].astype(BF16)
    wd_out_ref[...] = wd_ref[...].astype(BF16)


def _norm_qkv_kernel(x_ref, g_ref, wqk_ref, wvt_ref, q_ref, k_ref, vt_ref):
    nq = A_HEADS * A_HEAD_DIM
    tm = x_ref.shape[0]
    xn = _rms(x_ref[...], g_ref[...]).astype(BF16)
    qk = jnp.dot(xn, wqk_ref[...], preferred_element_type=F32)
    q_ref[...] = (qk[:, :nq] * (A_HEAD_DIM ** -0.5 * LOG2E)).astype(BF16)
    low = lax.broadcasted_iota(jnp.int32, (1, LANES), 1) < A_HEAD_DIM
    for kh in range(A_KV_HEADS):
        pair = qk[:, nq + (kh // 2) * LANES:nq + (kh // 2 + 1) * LANES]
        swap = pltpu.roll(pair, A_HEAD_DIM, 1)
        own, other = (pair, swap) if kh % 2 == 0 else (swap, pair)
        k_ref[kh, 0] = jnp.where(low, own, 0.0).astype(BF16)
        k_ref[kh, 1] = jnp.where(low, 0.0, other).astype(BF16)
    vt = lax.dot_general(wvt_ref[...], xn, NT_DIMS, preferred_element_type=F32)
    ones = jnp.ones((ONES_ROWS, BLK), BF16)
    for kh in range(A_KV_HEADS):
        for tb in range(tm // BLK):
            vt_ref[kh, tb, :A_HEAD_DIM, :] = (
                vt[kh * A_HEAD_DIM:(kh + 1) * A_HEAD_DIM, tb * BLK:(tb + 1) * BLK].astype(BF16))
            vt_ref[kh, tb, A_HEAD_DIM:, :] = ones


def _norm_qkv(x, g_all, layer, wqk_all, wvt_all, mixer):
    T, D = x.shape
    nq = A_HEADS * A_HEAD_DIM
    tm = TM_PROJ
    return pl.pallas_call(
        _norm_qkv_kernel,
        grid=(T // tm,),
        in_specs=[pl.BlockSpec((tm, D), lambda i: (i, 0)),
                  _layer_spec(g_all, layer),
                  _layer_spec(wqk_all, mixer),
                  _layer_spec(wvt_all, mixer)],
        out_specs=[pl.BlockSpec((tm, nq), lambda i: (i, 0)),
                   pl.BlockSpec((A_KV_HEADS, 2, tm, LANES), lambda i: (0, 0, i, 0)),
                   pl.BlockSpec((A_KV_HEADS, tm // BLK, A_HEAD_DIM + ONES_ROWS, BLK),
                                lambda i: (0, i, 0, 0))],
        out_shape=[jax.ShapeDtypeStruct((T, nq), BF16),
                   jax.ShapeDtypeStruct((A_KV_HEADS, 2, T, LANES), BF16),
                   jax.ShapeDtypeStruct((A_KV_HEADS, T // BLK, A_HEAD_DIM + ONES_ROWS, BLK), BF16)],
        compiler_params=_params(1),
        name="norm_qkv",
    )(x, g_all, wqk_all, wvt_all)


def _alibi_slope(h):
    return float(2.0 ** (-8.0 * (h + 1) / A_HEADS))


def _win_attn_kernel(sink_ref, q_ref, k_ref, vt_ref, pc_ref, pr_ref, wgu_ref, wd_ref,
                     o_ref, wgu_out_ref, wd_out_ref, s0_sc, m0_sc, s1_sc, m1_sc, *, n_blocks):
    _cast_ffn_weight_slab(wgu_ref, wd_ref, wgu_out_ref, wd_out_ref)
    nkey = 3 * BLK
    r = lax.broadcasted_iota(jnp.int32, (nkey, BLK), 0)
    c = lax.broadcasted_iota(jnp.int32, (nkey, BLK), 1)
    rc = r - c

    def window(blk):
        if isinstance(blk, int):
            return min(max(blk - 1, 0), n_blocks - 3)
        return jnp.clip(blk - 1, 0, n_blocks - 3)

    def row_start(blk):
        return blk * BLK if isinstance(blk, int) else pl.multiple_of(blk * BLK, BLK)

    def masked_distance(blk):
        wb = window(blk)
        valid = jnp.abs(rc + (wb - blk) * BLK) <= WINDOW
        kpos = pc_ref[pl.ds(row_start(wb), nkey), :]
        qpos = pr_ref[pl.ds(blk, 1), :]
        return jnp.where(valid, jnp.abs(kpos - qpos).astype(F32), MASK_DIST)

    def scores(blk, kh, dm, s_sc, m_sc):
        r0 = row_start(blk)
        w0 = row_start(window(blk))
        col = kh * A_GROUP * A_HEAD_DIM
        qst = jnp.concatenate([q_ref[pl.ds(r0, BLK), col:col + LANES],
                               q_ref[pl.ds(r0, BLK), col + LANES:col + 2 * LANES]], axis=0)
        for part in range(2):
            st = lax.dot_general(k_ref[kh, part, pl.ds(w0, nkey), :], qst, NT_DIMS,
                                 preferred_element_type=F32)
            for half in range(2):
                h = kh * A_GROUP + 2 * half + part
                s = st[:, half * BLK:(half + 1) * BLK] - (_alibi_slope(h) * LOG2E) * dm
                s_sc[h] = s
                m = jnp.maximum(jnp.max(s, axis=0, keepdims=True), sink_ref[h] * LOG2E)
                m_sc[h] = jnp.broadcast_to(m, (8, BLK))

    def context(blk, kh, s_sc, m_sc):
        r0 = row_start(blk)
        wb = window(blk)
        col = kh * A_GROUP * A_HEAD_DIM
        vt = jnp.concatenate([vt_ref[kh, wb + t] for t in range(3)], axis=1)
        normed = [None] * A_GROUP
        for part in range(2):
            heads = [kh * A_GROUP + 2 * half + part for half in range(2)]
            pt = jnp.concatenate(
                [jnp.exp2(s_sc[h] - m_sc[h][0:1, :]).astype(BF16) for h in heads], axis=1)
            ot = jnp.dot(vt, pt, preferred_element_type=F32)
            for half, h in enumerate(heads):
                cols = slice(half * BLK, (half + 1) * BLK)
                l = ot[A_HEAD_DIM:A_HEAD_DIM + 1, cols] + jnp.exp2(
                    sink_ref[h] * LOG2E - m_sc[h][0:1, :])
                normed[2 * half + part] = ot[:A_HEAD_DIM, cols] / l
        for half in range(A_GROUP // 2):
            pair_t = jnp.concatenate([normed[2 * half], normed[2 * half + 1]], axis=0)
            o_ref[pl.ds(r0, BLK), col + half * LANES:col + (half + 1) * LANES] = (
                pair_t.T.astype(BF16))

    def stage(ctx, sc, s_old, m_old, s_new, m_new):
        dm = None if sc is None else masked_distance(sc)
        for kh in range(A_KV_HEADS):
            if ctx is not None:
                context(ctx, kh, s_old, m_old)
            if sc is not None:
                scores(sc, kh, dm, s_new, m_new)

    stage(None, 0, None, None, s0_sc, m0_sc)

    def body(j, carry):
        stage(2 * j, 2 * j + 1, s0_sc, m0_sc, s1_sc, m1_sc)
        stage(2 * j + 1, 2 * j + 2, s1_sc, m1_sc, s0_sc, m0_sc)
        return carry

    lax.fori_loop(0, n_blocks // 2 - 1, body, 0)
    stage(n_blocks - 2, n_blocks - 1, s0_sc, m0_sc, s1_sc, m1_sc)
    stage(n_blocks - 1, None, s1_sc, m1_sc, None, None)


def _win_attn(q, kcat, vt, pos_col, pos_blk, sink, wgu_all, wd_all, layer):
    B, S, nq = q.shape
    nb = S // BLK
    s_buf = pltpu.VMEM((A_HEADS, 3 * BLK, BLK), F32)
    m_buf = pltpu.VMEM((A_HEADS, 8, BLK), F32)
    slab_in, slab_out, slab_shapes = _ffn_weight_slabs(wgu_all, wd_all, layer, B,
                                                       lambda b: b)
    return pl.pallas_call(
        functools.partial(_win_attn_kernel, n_blocks=nb),
        grid=(B,),
        in_specs=[pl.BlockSpec(memory_space=pltpu.SMEM),
                  pl.BlockSpec((None, S, nq), lambda b: (b, 0, 0)),
                  pl.BlockSpec((A_KV_HEADS, 2, S, LANES), lambda b: (0, 0, b, 0)),
                  pl.BlockSpec((A_KV_HEADS, nb, A_HEAD_DIM + ONES_ROWS, BLK), lambda b: (0, b, 0, 0)),
                  pl.BlockSpec((None, S, 1), lambda b: (b, 0, 0)),
                  pl.BlockSpec((None, nb, BLK), lambda b: (b, 0, 0))] + slab_in,
        out_specs=[pl.BlockSpec((None, S, nq), lambda b: (b, 0, 0))] + slab_out,
        out_shape=[jax.ShapeDtypeStruct((B, S, nq), BF16)] + slab_shapes,
        scratch_shapes=[s_buf, m_buf, s_buf, m_buf],
        compiler_params=_params(1),
        name="win_attn",
    )(sink, q, kcat, vt, pos_col, pos_blk, wgu_all, wd_all)


def _proj_ffn_kernel(h_ref, a_ref, wo_ref, g_ref, wgu_ref, wd_ref, gf_ref, out_ref, act_sc,
                     *, final_norm):
    h1 = h_ref[...] + jnp.dot(a_ref[...], wo_ref[...], preferred_element_type=F32)
    xn = _rms(h1, g_ref[...]).astype(BF16)
    for c in range(D_FF // TF_FFN):
        lo = c * TF_FFN
        gate = jnp.dot(xn, wgu_ref[:, lo:lo + TF_FFN], preferred_element_type=F32)
        up = jnp.dot(xn, wgu_ref[:, D_FF + lo:D_FF + lo + TF_FFN], preferred_element_type=F32)
        act_sc[:, lo:lo + TF_FFN] = (gate * jax.nn.sigmoid(gate) * up).astype(BF16)
    y = h1 + jnp.dot(act_sc[...], wd_ref[...], preferred_element_type=F32)
    if final_norm:
        y = _rms(y, gf_ref[...])
    out_ref[...] = y


def _proj_ffn(h, a, wo_all, mixer, g_all, wgu, wd, layer, gf, final_norm):
    T, D = h.shape
    tm = TM_FFN
    row = lambda i: (i, 0)
    const = lambda i: (0, 0)
    resident = pl.Buffered(1)
    return pl.pallas_call(
        functools.partial(_proj_ffn_kernel, final_norm=final_norm),
        grid=(T // tm,),
        in_specs=[pl.BlockSpec((tm, D), row),
                  pl.BlockSpec((tm, a.shape[1]), row),
                  _layer_spec(wo_all, mixer, pipeline_mode=resident),
                  _layer_spec(g_all, layer),
                  pl.BlockSpec(wgu.shape, const, pipeline_mode=resident),
                  pl.BlockSpec(wd.shape, const, pipeline_mode=resident),
                  pl.BlockSpec((1, D), const)],
        out_specs=pl.BlockSpec((tm, D), row),
        out_shape=jax.ShapeDtypeStruct((T, D), F32),
        scratch_shapes=[pltpu.VMEM((tm, D_FF), BF16)],
        compiler_params=_params(1),
        name="proj_ffn",
    )(h, a, wo_all, g_all, wgu, wd, gf)


TOK_PER_ROW = LANES // ROPE_HALF


def _rope_table_kernel(pos_ref, inv_ref, ct_ref, sg_ref):
    ang = pos_ref[...].astype(F32) * inv_ref[...]
    cos = jnp.cos(ang)
    sin = jnp.sin(ang)
    rows = ang.shape[0]
    lane = lax.broadcasted_iota(jnp.int32, ang.shape, 1)
    first = lane < ROPE_HALF
    second = (lane >= ROPE_PARTNER) & (lane < ROPE_PARTNER + ROPE_HALF)
    for g in range(TOK_PER_ROW):
        to_first = (LANES - g * ROPE_HALF) % LANES
        to_second = (to_first + ROPE_PARTNER) % LANES
        c1, c2 = (cos if s == 0 else pltpu.roll(cos, s, 1) for s in (to_first, to_second))
        s1, s2 = (sin if s == 0 else pltpu.roll(sin, s, 1) for s in (to_first, to_second))
        tokens = pl.ds(g, rows, stride=TOK_PER_ROW)
        ct_ref[tokens, :] = jnp.where(first, c1, jnp.where(second, c2, 1.0))
        sg_ref[tokens, :] = jnp.where(first, -s1, jnp.where(second, s2, 0.0))


def _rope_tables(pos_dense, inv_dense):
    T = pos_dense.shape[0] * TOK_PER_ROW
    tm = 2048
    tab = jax.ShapeDtypeStruct((T, LANES), F32)
    return pl.pallas_call(
        _rope_table_kernel,
        grid=(T // tm,),
        in_specs=[pl.BlockSpec((tm // TOK_PER_ROW, LANES), lambda i: (i, 0)),
                  pl.BlockSpec((1, LANES), lambda i: (0, 0))],
        out_specs=[pl.BlockSpec((tm, LANES), lambda i: (i, 0))] * 2,
        out_shape=[tab, tab],
        compiler_params=_params(1),
        name="rope_tables",
    )(pos_dense, inv_dense)


def _mla_prep_kernel(x_ref, g_ref, win_ref, gq_ref, gkv_ref, wuq_ref, wuk_ref, wuvt_ref,
                     ct_ref, sg_ref, q_ref, k_ref, vt_ref):
    scale = (NOPE_DIM + ROPE_DIM) ** -0.5 * LOG2E
    for part in range(TM_PREP // SUB_PREP):
        rows = slice(part * SUB_PREP, (part + 1) * SUB_PREP)
        xn = _rms(x_ref[rows, :], g_ref[...]).astype(BF16)
        lat = jnp.dot(xn, win_ref[...], preferred_element_type=F32)
        cq = _rms(lat[:, :Q_LORA], gq_ref[...]).astype(BF16)
        ckv = _rms(lat[:, Q_LORA:Q_LORA + KV_LORA], gkv_ref[...]).astype(BF16)
        ct = ct_ref[rows, :]
        sg = sg_ref[rows, :]
        k_rope = lat[:, Q_LORA + KV_LORA:]
        k_rope = k_rope * ct + pltpu.roll(k_rope, ROPE_PARTNER, 1) * sg
        ctq = ct * scale
        sgq = sg * scale
        q = jnp.dot(cq, wuq_ref[...], preferred_element_type=F32)
        kn = jnp.dot(ckv, wuk_ref[...], preferred_element_type=F32)
        vt = lax.dot_general(wuvt_ref[...], ckv, NT_DIMS, preferred_element_type=F32)
        ones = jnp.ones((ONES_ROWS, SUB_PREP), BF16)
        for h in range(B_HEADS):
            sl = slice(h * LANES, (h + 1) * LANES)
            qh = q[:, sl]
            q_ref[h, rows, :] = (qh * ctq + pltpu.roll(qh, ROPE_PARTNER, 1) * sgq).astype(BF16)
            k_ref[h, rows, :] = (kn[:, sl] + k_rope).astype(BF16)
            vt_ref[h, :V_DIM, rows] = vt[h * V_DIM:(h + 1) * V_DIM, :].astype(BF16)
            vt_ref[h, V_DIM:, rows] = ones


def _mla_prep(x, g_all, layer, win, gq, gkv, wuq, wuk, wuvt, mixer, ct, sg):
    B, S, D = x.shape
    tm = TM_PREP
    row = lambda b, i: (b, i, 0)
    head = lambda b, i: (b, 0, i, 0)
    head_t = lambda b, i: (b, 0, 0, i)
    qk_shape = jax.ShapeDtypeStruct((B, B_HEADS, S, LANES), BF16)
    return pl.pallas_call(
        _mla_prep_kernel,
        grid=(B, S // tm),
        in_specs=[pl.BlockSpec((None, tm, D), row),
                  _layer_spec(g_all, layer),
                  _layer_spec(win, mixer),
                  _layer_spec(gq, mixer),
                  _layer_spec(gkv, mixer),
                  _layer_spec(wuq, mixer),
                  _layer_spec(wuk, mixer),
                  _layer_spec(wuvt, mixer),
                  pl.BlockSpec((None, tm, LANES), row),
                  pl.BlockSpec((None, tm, LANES), row)],
        out_specs=[pl.BlockSpec((None, B_HEADS, tm, LANES), head),
                   pl.BlockSpec((None, B_HEADS, tm, LANES), head),
                   pl.BlockSpec((None, B_HEADS, V_DIM + ONES_ROWS, tm), head_t)],
        out_shape=[qk_shape, qk_shape,
                   jax.ShapeDtypeStruct((B, B_HEADS, V_DIM + ONES_ROWS, S), BF16)],
        compiler_params=_params(2),
        name="mla_prep",
    )(x, g_all, win, gq, gkv, wuq, wuk, wuvt, ct, sg)


def _mla_attn_kernel(q_ref, k_ref, vt_ref, wgu_ref, wd_ref, o_ref, wgu_out_ref, wd_out_ref,
                     s0_sc, m0_sc, s1_sc, m1_sc, acc_sc):
    _cast_ffn_weight_slab(wgu_ref, wd_ref, wgu_out_ref, wd_out_ref)
    S = q_ref.shape[1]
    n = S // TQ_MLA

    def row_start(blk):
        if isinstance(blk, int):
            return blk * TQ_MLA
        return pl.multiple_of(blk * TQ_MLA, TQ_MLA)

    def finish(item):
        hp, blk = item
        outs = [acc_sc[e][:V_DIM] / acc_sc[e][V_DIM:V_DIM + 1] for e in range(2)]
        o_ref[pl.ds(row_start(blk), TQ_MLA), hp * LANES:(hp + 1) * LANES] = (
            jnp.concatenate(outs, axis=0).T.astype(BF16))

    def stage(fin, ctx, sc, s_old, m_old, s_new, m_new):
        if fin is not None:
            finish(fin)
        for e in range(2):
            if sc is not None:
                head_sc = 2 * sc[0] + e
                q = q_ref[head_sc, pl.ds(row_start(sc[1]), TQ_MLA), :]
                m_run = None
            if ctx is not None:
                head_ctx = 2 * ctx[0] + e
                m = m_old[e][0:1, :]
                acc = None
            for c in range(S // KC_MLA):
                keys = slice(c * KC_MLA, (c + 1) * KC_MLA)
                if ctx is not None:
                    pt = jnp.exp2(s_old[e, keys, :] - m).astype(BF16)
                    part = jnp.dot(vt_ref[head_ctx, :, keys], pt, preferred_element_type=F32)
                    acc = part if acc is None else acc + part
                if sc is not None:
                    st = lax.dot_general(k_ref[head_sc, keys, :], q, NT_DIMS,
                                         preferred_element_type=F32)
                    s_new[e, keys, :] = st
                    m_chunk = jnp.max(st, axis=0, keepdims=True)
                    m_run = m_chunk if m_run is None else jnp.maximum(m_run, m_chunk)
            if sc is not None:
                m_new[e] = jnp.broadcast_to(m_run, (8, TQ_MLA))
            if ctx is not None:
                acc_sc[e] = acc

    n_items = PAIRS_MLA * n
    scratch = ((s0_sc, m0_sc), (s1_sc, m1_sc))

    def item(t):
        return (t // n, t % n) if 0 <= t < n_items else None

    def static_stage(k):
        stage(item(k - 2), item(k - 1), item(k), *scratch[(k - 1) % 2], *scratch[k % 2])

    for hp in range(PAIRS_MLA):
        static_stage(hp * n)
        static_stage(hp * n + 1)

        def body(j, carry, hp=hp):
            stage((hp, 2 * j), (hp, 2 * j + 1), (hp, 2 * j + 2), *scratch[1], *scratch[0])
            stage((hp, 2 * j + 1), (hp, 2 * j + 2), (hp, 2 * j + 3), *scratch[0], *scratch[1])
            return carry

        lax.fori_loop(0, n // 2 - 1, body, 0)
    static_stage(n_items)
    static_stage(n_items + 1)


def _mla_attn(q, k, v, wgu_all, wd_all, layer):
    B, H, S, _ = q.shape
    heads = 2 * PAIRS_MLA
    groups = H // heads
    pair = lambda b, g: (b, g, 0, 0)
    s_buf = pltpu.VMEM((2, S, TQ_MLA), F32)
    m_buf = pltpu.VMEM((2, 8, TQ_MLA), F32)
    slab_in, slab_out, slab_shapes = _ffn_weight_slabs(wgu_all, wd_all, layer, B * groups,
                                                       lambda b, g: b * groups + g)
    return pl.pallas_call(
        _mla_attn_kernel,
        grid=(B, groups),
        in_specs=[pl.BlockSpec((None, heads, S, LANES), pair),
                  pl.BlockSpec((None, heads, S, LANES), pair),
                  pl.BlockSpec((None, heads, V_DIM + ONES_ROWS, S), pair)] + slab_in,
        out_specs=[pl.BlockSpec((None, S, PAIRS_MLA * LANES), lambda b, g: (b, 0, g))] + slab_out,
        out_shape=[jax.ShapeDtypeStruct((B, S, H * V_DIM), BF16)] + slab_shapes,
        scratch_shapes=[s_buf, m_buf, s_buf, m_buf,
                        pltpu.VMEM((2, V_DIM + ONES_ROWS, TQ_MLA), F32)],
        compiler_params=_params(2),
        name="mla_attn",
    )(q, k, v, wgu_all, wd_all)


def _head_block(nope, rope):
    split = ROPE_PARTNER - ROPE_HALF
    pad = jnp.zeros(nope.shape[:-1] + (LANES - NOPE_DIM - ROPE_DIM,), nope.dtype)
    w = jnp.concatenate([rope[..., :ROPE_HALF], nope[..., :split],
                         rope[..., ROPE_HALF:], nope[..., split:], pad], axis=-1)
    return w.reshape(w.shape[:-2] + (w.shape[-2] * LANES,))


def kernel(x, positions, norm_mix, norm_ffn, a_w_qkv, a_sink, a_w_o, b_w_in, b_g_q, b_g_kv,
           b_w_uq, b_w_ukv, b_w_o, ffn_w_gu, ffn_w_down, final_norm):
    B, S, D = x.shape
    T = B * S
    n_a = a_w_qkv.shape[0]
    n_b = b_w_in.shape[0]
    h = x.reshape(T, D)
    pos_col3 = positions.reshape(B, S, 1)
    pos_blk3 = positions.reshape(B, S // BLK, BLK)

    inv_freq = ROPE_THETA ** (-jnp.arange(ROPE_HALF, dtype=F32) * 2.0 / ROPE_DIM)
    pos_dense = jnp.repeat(positions.reshape(T // TOK_PER_ROW, TOK_PER_ROW), ROPE_HALF, axis=1)
    ct, sg = _rope_tables(pos_dense, jnp.tile(inv_freq, TOK_PER_ROW).reshape(1, LANES))
    ct, sg = (t.reshape(B, S, LANES) for t in (ct, sg))

    g_mix = norm_mix.reshape(DEPTH, 1, D)
    g_ffn = norm_ffn.reshape(DEPTH, 1, D)
    gf = final_norm.reshape(1, D)

    nq = A_HEADS * A_HEAD_DIM
    nqk = nq + A_KV_HEADS * A_HEAD_DIM
    a_wqk = a_w_qkv[:, :, :nqk].astype(BF16)
    a_wvt = jnp.swapaxes(a_w_qkv[:, :, nqk:], 1, 2).astype(BF16)
    a_wo = a_w_o.astype(BF16)

    n_lat = Q_LORA + KV_LORA
    w_kr = _head_block(jnp.zeros((n_b, D, 1, NOPE_DIM), F32),
                       b_w_in[:, :, n_lat:].reshape(n_b, D, 1, ROPE_DIM))
    b_win = jnp.concatenate([b_w_in[:, :, :n_lat], w_kr], axis=-1).astype(BF16)
    w_uq = b_w_uq.reshape(n_b, Q_LORA, B_HEADS, NOPE_DIM + ROPE_DIM)
    b_wuq = _head_block(w_uq[..., :NOPE_DIM], w_uq[..., NOPE_DIM:]).astype(BF16)
    w_ukv = b_w_ukv.reshape(n_b, KV_LORA, B_HEADS, NOPE_DIM + V_DIM)
    b_wuk = _head_block(w_ukv[..., :NOPE_DIM],
                        jnp.zeros((n_b, KV_LORA, B_HEADS, ROPE_DIM), F32)).astype(BF16)
    b_wuvt = jnp.swapaxes(w_ukv[..., NOPE_DIM:].reshape(n_b, KV_LORA, B_HEADS * V_DIM),
                          1, 2).astype(BF16)
    b_gq = b_g_q.reshape(n_b, 1, Q_LORA)
    b_gkv = b_g_kv.reshape(n_b, 1, KV_LORA)
    b_wo = b_w_o.astype(BF16)

    for i in range(DEPTH):
        j = i // 2
        if i % 2 == 0:
            q, kcat, vt = _norm_qkv(h, g_mix, i, a_wqk, a_wvt, j)
            attn, w_gu, w_down = _win_attn(q.reshape(B, S, nq), kcat, vt, pos_col3, pos_blk3,
                                           a_sink[j], ffn_w_gu, ffn_w_down, i)
            attn = attn.reshape(T, nq)
            w_o = a_wo
        else:
            qh, kh, v = _mla_prep(h.reshape(B, S, D), g_mix, i, b_win, b_gq, b_gkv,
                                  b_wuq, b_wuk, b_wuvt, j, ct, sg)
            attn, w_gu, w_down = _mla_attn(qh, kh, v, ffn_w_gu, ffn_w_down, i)
            attn = attn.reshape(T, B_HEADS * V_DIM)
            w_o = b_wo
        h = _proj_ffn(h, attn, w_o, j, g_ffn, w_gu, w_down, i, gf,
                      final_norm=(i == DEPTH - 1))
    return h.reshape(B, S, D)
```

```python
import functools
import math

import jax
import jax.numpy as jnp
from jax import lax
from jax.experimental import pallas as pl
from jax.experimental.pallas import tpu as pltpu

F32 = jnp.float32
BF16 = jnp.bfloat16

D_MODEL = 1024
DEPTH = 4
A_HEADS = 16
A_KV_HEADS = 4
A_GROUP = A_HEADS // A_KV_HEADS
A_HEAD_DIM = 64
WINDOW = 128
BLK = 128
B_HEADS = 16
Q_LORA = 384
KV_LORA = 256
NOPE_DIM = 64
ROPE_DIM = 32
V_DIM = 64
ROPE_THETA = 10000.0
D_FF = 2816
EPS = 1e-6

LOG2E = math.log2(math.e)
MASK_DIST = 1e36

LANES = 128
ONES_ROWS = 64
VMEM_LIMIT = 56 * 1024 * 1024

TM_PROJ = 1024
TM_FFN = 1024
TF_FFN = 256
TM_PREP = 1024
SUB_PREP = 256

ROPE_HALF = ROPE_DIM // 2
ROPE_PARTNER = LANES // 2
TQ_MLA = 256
KC_MLA = 256
PAIRS_MLA = 4

NT_DIMS = (((1,), (1,)), ((), ()))


def _rms(x, g):
    ms = jnp.mean(x * x, axis=-1, keepdims=True)
    return x * lax.rsqrt(ms + EPS) * g


def _params(n_axes):
    return pltpu.CompilerParams(dimension_semantics=("arbitrary",) * n_axes,
                                vmem_limit_bytes=VMEM_LIMIT)


def _layer_spec(stacked, layer, **kwargs):
    index = (layer,) + (0,) * (stacked.ndim - 1)
    return pl.BlockSpec((None,) + stacked.shape[1:], lambda *_: index, **kwargs)


def _ffn_weight_slabs(wgu_all, wd_all, layer, n_steps, step_of):
    specs_in, specs_out, shapes = [], [], []
    for w_all in (wgu_all, wd_all):
        rows, cols = w_all.shape[1:]
        slab = rows // n_steps
        specs_in.append(pl.BlockSpec((None, slab, cols),
                                     lambda *idx: (layer, step_of(*idx), 0)))
        specs_out.append(pl.BlockSpec((slab, cols), lambda *idx: (step_of(*idx), 0)))
        shapes.append(jax.ShapeDtypeStruct((rows, cols), BF16))
    return specs_in, specs_out, shapes


def _cast_ffn_weight_slab(wgu_ref, wd_ref, wgu_out_ref, wd_out_ref):
    wgu_out_ref[...] = wgu_ref[...].astype(BF16)
    wd_out_ref[...] = wd_ref[...].astype(BF16)


def _norm_qkv_kernel(x_ref, g_ref, wqk_ref, wvt_ref, q_ref, k_ref, vt_ref):
    nq = A_HEADS * A_HEAD_DIM
    tm = x_ref.shape[0]
    xn = _rms(x_ref[...], g_ref[...]).astype(BF16)
    qk = jnp.dot(xn, wqk_ref[...], preferred_element_type=F32)
    q_ref[...] = (qk[:, :nq] * (A_HEAD_DIM ** -0.5 * LOG2E)).astype(BF16)
    low = lax.broadcasted_iota(jnp.int32, (1, LANES), 1) < A_HEAD_DIM
    for kh in range(A_KV_HEADS):
        pair = qk[:, nq + (kh // 2) * LANES:nq + (kh // 2 + 1) * LANES]
        swap = pltpu.roll(pair, A_HEAD_DIM, 1)
        own, other = (pair, swap) if kh % 2 == 0 else (swap, pair)
        k_ref[kh, 0] = jnp.where(low, own, 0.0).astype(BF16)
        k_ref[kh, 1] = jnp.where(low, 0.0, other).astype(BF16)
    vt = lax.dot_general(wvt_ref[...], xn, NT_DIMS, preferred_element_type=F32)
    ones = jnp.ones((ONES_ROWS, BLK), BF16)
    for kh in range(A_KV_HEADS):
        for tb in range(tm // BLK):
            vt_ref[kh, tb, :A_HEAD_DIM, :] = (
                vt[kh * A_HEAD_DIM:(kh + 1) * A_HEAD_DIM, tb * BLK:(tb + 1) * BLK].astype(BF16))
            vt_ref[kh, tb, A_HEAD_DIM:, :] = ones


def _norm_qkv(x, g_all, layer, wqk_all, wvt_all, mixer):
    T, D = x.shape
    nq = A_HEADS * A_HEAD_DIM
    tm = TM_PROJ
    return pl.pallas_call(
        _norm_qkv_kernel,
        grid=(T // tm,),
        in_specs=[pl.BlockSpec((tm, D), lambda i: (i, 0)),
                  _layer_spec(g_all, layer),
                  _layer_spec(wqk_all, mixer),
                  _layer_spec(wvt_all, mixer)],
        out_specs=[pl.BlockSpec((tm, nq), lambda i: (i, 0)),
                   pl.BlockSpec((A_KV_HEADS, 2, tm, LANES), lambda i: (0, 0, i, 0)),
                   pl.BlockSpec((A_KV_HEADS, tm // BLK, A_HEAD_DIM + ONES_ROWS, BLK),
                                lambda i: (0, i, 0, 0))],
        out_shape=[jax.ShapeDtypeStruct((T, nq), BF16),
                   jax.ShapeDtypeStruct((A_KV_HEADS, 2, T, LANES), BF16),
                   jax.ShapeDtypeStruct((A_KV_HEADS, T // BLK, A_HEAD_DIM + ONES_ROWS, BLK), BF16)],
        compiler_params=_params(1),
        name="norm_qkv",
    )(x, g_all, wqk_all, wvt_all)


def _alibi_slope(h):
    return float(2.0 ** (-8.0 * (h + 1) / A_HEADS))


def _win_attn_kernel(sink_ref, q_ref, k_ref, vt_ref, pc_ref, pr_ref, wgu_ref, wd_ref,
                     o_ref, wgu_out_ref, wd_out_ref, s0_sc, m0_sc, s1_sc, m1_sc, *, n_blocks):
    _cast_ffn_weight_slab(wgu_ref, wd_ref, wgu_out_ref, wd_out_ref)
    nkey = 3 * BLK
    r = lax.broadcasted_iota(jnp.int32, (nkey, BLK), 0)
    c = lax.broadcasted_iota(jnp.int32, (nkey, BLK), 1)
    rc = r - c

    def window(blk):
        if isinstance(blk, int):
            return min(max(blk - 1, 0), n_blocks - 3)
        return jnp.clip(blk - 1, 0, n_blocks - 3)

    def row_start(blk):
        return blk * BLK if isinstance(blk, int) else pl.multiple_of(blk * BLK, BLK)

    def masked_distance(blk):
        wb = window(blk)
        valid = jnp.abs(rc + (wb - blk) * BLK) <= WINDOW
        kpos = pc_ref[pl.ds(row_start(wb), nkey), :]
        qpos = pr_ref[pl.ds(blk, 1), :]
        return jnp.where(valid, jnp.abs(kpos - qpos).astype(F32), MASK_DIST)

    def scores(blk, kh, dm, s_sc, m_sc):
        r0 = row_start(blk)
        w0 = row_start(window(blk))
        col = kh * A_GROUP * A_HEAD_DIM
        qst = jnp.concatenate([q_ref[pl.ds(r0, BLK), col:col + LANES],
                               q_ref[pl.ds(r0, BLK), col + LANES:col + 2 * LANES]], axis=0)
        for part in range(2):
            st = lax.dot_general(k_ref[kh, part, pl.ds(w0, nkey), :], qst, NT_DIMS,
                                 preferred_element_type=F32)
            for half in range(2):
                h = kh * A_GROUP + 2 * half + part
                s = st[:, half * BLK:(half + 1) * BLK] - (_alibi_slope(h) * LOG2E) * dm
                s_sc[h] = s
                m = jnp.maximum(jnp.max(s, axis=0, keepdims=True), sink_ref[h] * LOG2E)
                m_sc[h] = jnp.broadcast_to(m, (8, BLK))

    def context(blk, kh, s_sc, m_sc):
        r0 = row_start(blk)
        wb = window(blk)
        col = kh * A_GROUP * A_HEAD_DIM
        vt = jnp.concatenate([vt_ref[kh, wb + t] for t in range(3)], axis=1)
        normed = [None] * A_GROUP
        for part in range(2):
            heads = [kh * A_GROUP + 2 * half + part for half in range(2)]
            pt = jnp.concatenate(
                [jnp.exp2(s_sc[h] - m_sc[h][0:1, :]).astype(BF16) for h in heads], axis=1)
            ot = jnp.dot(vt, pt, preferred_element_type=F32)
            for half, h in enumerate(heads):
                cols = slice(half * BLK, (half + 1) * BLK)
                l = ot[A_HEAD_DIM:A_HEAD_DIM + 1, cols] + jnp.exp2(
                    sink_ref[h] * LOG2E - m_sc[h][0:1, :])
                normed[2 * half + part] = ot[:A_HEAD_DIM, cols] / l
        for half in range(A_GROUP // 2):
            pair_t = jnp.concatenate([normed[2 * half], normed[2 * half + 1]], axis=0)
            o_ref[pl.ds(r0, BLK), col + half * LANES:col + (half + 1) * LANES] = (
                pair_t.T.astype(BF16))

    def stage(ctx, sc, s_old, m_old, s_new, m_new):
        dm = None if sc is None else masked_distance(sc)
        for kh in range(A_KV_HEADS):
            if ctx is not None:
                context(ctx, kh, s_old, m_old)
            if sc is not None:
                scores(sc, kh, dm, s_new, m_new)

    stage(None, 0, None, None, s0_sc, m0_sc)

    def body(j, carry):
        stage(2 * j, 2 * j + 1, s0_sc, m0_sc, s1_sc, m1_sc)
        stage(2 * j + 1, 2 * j + 2, s1_sc, m1_sc, s0_sc, m0_sc)
        return carry

    lax.fori_loop(0, n_blocks // 2 - 1, body, 0)
    stage(n_blocks - 2, n_blocks - 1, s0_sc, m0_sc, s1_sc, m1_sc)
    stage(n_blocks - 1, None, s1_sc, m1_sc, None, None)


def _win_attn(q, kcat, vt, pos_col, pos_blk, sink, wgu_all, wd_all, layer):
    B, S, nq = q.shape
    nb = S // BLK
    s_buf = pltpu.VMEM((A_HEADS, 3 * BLK, BLK), F32)
    m_buf = pltpu.VMEM((A_HEADS, 8, BLK), F32)
    slab_in, slab_out, slab_shapes = _ffn_weight_slabs(wgu_all, wd_all, layer, B,
                                                       lambda b: b)
    return pl.pallas_call(
        functools.partial(_win_attn_kernel, n_blocks=nb),
        grid=(B,),
        in_specs=[pl.BlockSpec(memory_space=pltpu.SMEM),
                  pl.BlockSpec((None, S, nq), lambda b: (b, 0, 0)),
                  pl.BlockSpec((A_KV_HEADS, 2, S, LANES), lambda b: (0, 0, b, 0)),
                  pl.BlockSpec((A_KV_HEADS, nb, A_HEAD_DIM + ONES_ROWS, BLK), lambda b: (0, b, 0, 0)),
                  pl.BlockSpec((None, S, 1), lambda b: (b, 0, 0)),
                  pl.BlockSpec((None, nb, BLK), lambda b: (b, 0, 0))] + slab_in,
        out_specs=[pl.BlockSpec((None, S, nq), lambda b: (b, 0, 0))] + slab_out,
        out_shape=[jax.ShapeDtypeStruct((B, S, nq), BF16)] + slab_shapes,
        scratch_shapes=[s_buf, m_buf, s_buf, m_buf],
        compiler_params=_params(1),
        name="win_attn",
    )(sink, q, kcat, vt, pos_col, pos_blk, wgu_all, wd_all)


def _proj_ffn_kernel(h_ref, a_ref, wo_ref, g_ref, wgu_ref, wd_ref, gf_ref, out_ref, act_sc,
                     *, final_norm):
    h1 = h_ref[...] + jnp.dot(a_ref[...], wo_ref[...], preferred_element_type=F32)
    xn = _rms(h1, g_ref[...]).astype(BF16)
    for c in range(D_FF // TF_FFN):
        lo = c * TF_FFN
        gate = jnp.dot(xn, wgu_ref[:, lo:lo + TF_FFN], preferred_element_type=F32)
        up = jnp.dot(xn, wgu_ref[:, D_FF + lo:D_FF + lo + TF_FFN], preferred_element_type=F32)
        act_sc[:, lo:lo + TF_FFN] = (gate * jax.nn.sigmoid(gate) * up).astype(BF16)
    y = h1 + jnp.dot(act_sc[...], wd_ref[...], preferred_element_type=F32)
    if final_norm:
        y = _rms(y, gf_ref[...])
    out_ref[...] = y


def _proj_ffn(h, a, wo_all, mixer, g_all, wgu, wd, layer, gf, final_norm):
    T, D = h.shape
    tm = TM_FFN
    row = lambda i: (i, 0)
    const = lambda i: (0, 0)
    resident = pl.Buffered(1)
    return pl.pallas_call(
        functools.partial(_proj_ffn_kernel, final_norm=final_norm),
        grid=(T // tm,),
        in_specs=[pl.BlockSpec((tm, D), row),
                  pl.BlockSpec((tm, a.shape[1]), row),
                  _layer_spec(wo_all, mixer, pipeline_mode=resident),
                  _layer_spec(g_all, layer),
                  pl.BlockSpec(wgu.shape, const, pipeline_mode=resident),
                  pl.BlockSpec(wd.shape, const, pipeline_mode=resident),
                  pl.BlockSpec((1, D), const)],
        out_specs=pl.BlockSpec((tm, D), row),
        out_shape=jax.ShapeDtypeStruct((T, D), F32),
        scratch_shapes=[pltpu.VMEM((tm, D_FF), BF16)],
        compiler_params=_params(1),
        name="proj_ffn",
    )(h, a, wo_all, g_all, wgu, wd, gf)


TOK_PER_ROW = LANES // ROPE_HALF


def _rope_table_kernel(pos_ref, inv_ref, ct_ref, sg_ref):
    ang = pos_ref[...].astype(F32) * inv_ref[...]
    cos = jnp.cos(ang)
    sin = jnp.sin(ang)
    rows = ang.shape[0]
    lane = lax.broadcasted_iota(jnp.int32, ang.shape, 1)
    first = lane < ROPE_HALF
    second = (lane >= ROPE_PARTNER) & (lane < ROPE_PARTNER + ROPE_HALF)
    for g in range(TOK_PER_ROW):
        to_first = (LANES - g * ROPE_HALF) % LANES
        to_second = (to_first + ROPE_PARTNER) % LANES
        c1, c2 = (cos if s == 0 else pltpu.roll(cos, s, 1) for s in (to_first, to_second))
        s1, s2 = (sin if s == 0 else pltpu.roll(sin, s, 1) for s in (to_first, to_second))
        tokens = pl.ds(g, rows, stride=TOK_PER_ROW)
        ct_ref[tokens, :] = jnp.where(first, c1, jnp.where(second, c2, 1.0))
        sg_ref[tokens, :] = jnp.where(first, -s1, jnp.where(second, s2, 0.0))


def _rope_tables(pos_dense, inv_dense):
    T = pos_dense.shape[0] * TOK_PER_ROW
    tm = 2048
    tab = jax.ShapeDtypeStruct((T, LANES), F32)
    return pl.pallas_call(
        _rope_table_kernel,
        grid=(T // tm,),
        in_specs=[pl.BlockSpec((tm // TOK_PER_ROW, LANES), lambda i: (i, 0)),
                  pl.BlockSpec((1, LANES), lambda i: (0, 0))],
        out_specs=[pl.BlockSpec((tm, LANES), lambda i: (i, 0))] * 2,
        out_shape=[tab, tab],
        compiler_params=_params(1),
        name="rope_tables",
    )(pos_dense, inv_dense)


def _mla_prep_kernel(x_ref, g_ref, win_ref, gq_ref, gkv_ref, wuq_ref, wuk_ref, wuvt_ref,
                     ct_ref, sg_ref, q_ref, k_ref, vt_ref):
    scale = (NOPE_DIM + ROPE_DIM) ** -0.5 * LOG2E
    for part in range(TM_PREP // SUB_PREP):
        rows = slice(part * SUB_PREP, (part + 1) * SUB_PREP)
        xn = _rms(x_ref[rows, :], g_ref[...]).astype(BF16)
        lat = jnp.dot(xn, win_ref[...], preferred_element_type=F32)
        cq = _rms(lat[:, :Q_LORA], gq_ref[...]).astype(BF16)
        ckv = _rms(lat[:, Q_LORA:Q_LORA + KV_LORA], gkv_ref[...]).astype(BF16)
        ct = ct_ref[rows, :]
        sg = sg_ref[rows, :]
        k_rope = lat[:, Q_LORA + KV_LORA:]
        k_rope = k_rope * ct + pltpu.roll(k_rope, ROPE_PARTNER, 1) * sg
        ctq = ct * scale
        sgq = sg * scale
        q = jnp.dot(cq, wuq_ref[...], preferred_element_type=F32)
        kn = jnp.dot(ckv, wuk_ref[...], preferred_element_type=F32)
        vt = lax.dot_general(wuvt_ref[...], ckv, NT_DIMS, preferred_element_type=F32)
        ones = jnp.ones((ONES_ROWS, SUB_PREP), BF16)
        for h in range(B_HEADS):
            sl = slice(h * LANES, (h + 1) * LANES)
            qh = q[:, sl]
            q_ref[h, rows, :] = (qh * ctq + pltpu.roll(qh, ROPE_PARTNER, 1) * sgq).astype(BF16)
            k_ref[h, rows, :] = (kn[:, sl] + k_rope).astype(BF16)
            vt_ref[h, :V_DIM, rows] = vt[h * V_DIM:(h + 1) * V_DIM, :].astype(BF16)
            vt_ref[h, V_DIM:, rows] = ones


def _mla_prep(x, g_all, layer, win, gq, gkv, wuq, wuk, wuvt, mixer, ct, sg):
    B, S, D = x.shape
    tm = TM_PREP
    row = lambda b, i: (b, i, 0)
    head = lambda b, i: (b, 0, i, 0)
    head_t = lambda b, i: (b, 0, 0, i)
    qk_shape = jax.ShapeDtypeStruct((B, B_HEADS, S, LANES), BF16)
    return pl.pallas_call(
        _mla_prep_kernel,
        grid=(B, S // tm),
        in_specs=[pl.BlockSpec((None, tm, D), row),
                  _layer_spec(g_all, layer),
                  _layer_spec(win, mixer),
                  _layer_spec(gq, mixer),
                  _layer_spec(gkv, mixer),
                  _layer_spec(wuq, mixer),
                  _layer_spec(wuk, mixer),
                  _layer_spec(wuvt, mixer),
                  pl.BlockSpec((None, tm, LANES), row),
                  pl.BlockSpec((None, tm, LANES), row)],
        out_specs=[pl.BlockSpec((None, B_HEADS, tm, LANES), head),
                   pl.BlockSpec((None, B_HEADS, tm, LANES), head),
                   pl.BlockSpec((None, B_HEADS, V_DIM + ONES_ROWS, tm), head_t)],
        out_shape=[qk_shape, qk_shape,
                   jax.ShapeDtypeStruct((B, B_HEADS, V_DIM + ONES_ROWS, S), BF16)],
        compiler_params=_params(2),
        name="mla_prep",
    )(x, g_all, win, gq, gkv, wuq, wuk, wuvt, ct, sg)


def _mla_attn_kernel(q_ref, k_ref, vt_ref, wgu_ref, wd_ref, o_ref, wgu_out_ref, wd_out_ref,
                     s0_sc, m0_sc, s1_sc, m1_sc, acc_sc):
    _cast_ffn_weight_slab(wgu_ref, wd_ref, wgu_out_ref, wd_out_ref)
    S = q_ref.shape[1]
    n = S // TQ_MLA

    def row_start(blk):
        if isinstance(blk, int):
            return blk * TQ_MLA
        return pl.multiple_of(blk * TQ_MLA, TQ_MLA)

    def finish(item):
        hp, blk = item
        outs = [acc_sc[e][:V_DIM] / acc_sc[e][V_DIM:V_DIM + 1] for e in range(2)]
        o_ref[pl.ds(row_start(blk), TQ_MLA), hp * LANES:(hp + 1) * LANES] = (
            jnp.concatenate(outs, axis=0).T.astype(BF16))

    def stage(fin, ctx, sc, s_old, m_old, s_new, m_new):
        if fin is not None:
            finish(fin)
        for e in range(2):
            if sc is not None:
                head_sc = 2 * sc[0] + e
                q = q_ref[head_sc, pl.ds(row_start(sc[1]), TQ_MLA), :]
                m_run = None
            if ctx is not None:
                head_ctx = 2 * ctx[0] + e
                m = m_old[e][0:1, :]
                acc = None
            for c in range(S // KC_MLA):
                keys = slice(c * KC_MLA, (c + 1) * KC_MLA)
                if ctx is not None:
                    pt = jnp.exp2(s_old[e, keys, :] - m).astype(BF16)
                    part = jnp.dot(vt_ref[head_ctx, :, keys], pt, preferred_element_type=F32)
                    acc = part if acc is None else acc + part
                if sc is not None:
                    st = lax.dot_general(k_ref[head_sc, keys, :], q, NT_DIMS,
                                         preferred_element_type=F32)
                    s_new[e, keys, :] = st
                    m_chunk = jnp.max(st, axis=0, keepdims=True)
                    m_run = m_chunk if m_run is None else jnp.maximum(m_run, m_chunk)
            if sc is not None:
                m_new[e] = jnp.broadcast_to(m_run, (8, TQ_MLA))
            if ctx is not None:
                acc_sc[e] = acc

    n_items = PAIRS_MLA * n
    scratch = ((s0_sc, m0_sc), (s1_sc, m1_sc))

    def item(t):
        return (t // n, t % n) if 0 <= t < n_items else None

    def static_stage(k):
        stage(item(k - 2), item(k - 1), item(k), *scratch[(k - 1) % 2], *scratch[k % 2])

    for hp in range(PAIRS_MLA):
        static_stage(hp * n)
        static_stage(hp * n + 1)

        def body(j, carry, hp=hp):
            stage((hp, 2 * j), (hp, 2 * j + 1), (hp, 2 * j + 2), *scratch[1], *scratch[0])
            stage((hp, 2 * j + 1), (hp, 2 * j + 2), (hp, 2 * j + 3), *scratch[0], *scratch[1])
            return carry

        lax.fori_loop(0, n // 2 - 1, body, 0)
    static_stage(n_items)
    static_stage(n_items + 1)


def _mla_attn(q, k, v, wgu_all, wd_all, layer):
    B, H, S, _ = q.shape
    heads = 2 * PAIRS_MLA
    groups = H // heads
    pair = lambda b, g: (b, g, 0, 0)
    s_buf = pltpu.VMEM((2, S, TQ_MLA), F32)
    m_buf = pltpu.VMEM((2, 8, TQ_MLA), F32)
    slab_in, slab_out, slab_shapes = _ffn_weight_slabs(wgu_all, wd_all, layer, B * groups,
                                                       lambda b, g: b * groups + g)
    return pl.pallas_call(
        _mla_attn_kernel,
        grid=(B, groups),
        in_specs=[pl.BlockSpec((None, heads, S, LANES), pair),
                  pl.BlockSpec((None, heads, S, LANES), pair),
                  pl.BlockSpec((None, heads, V_DIM + ONES_ROWS, S), pair)] + slab_in,
        out_specs=[pl.BlockSpec((None, S, PAIRS_MLA * LANES), lambda b, g: (b, 0, g))] + slab_out,
        out_shape=[jax.ShapeDtypeStruct((B, S, H * V_DIM), BF16)] + slab_shapes,
        scratch_shapes=[s_buf, m_buf, s_buf, m_buf,
                        pltpu.VMEM((2, V_DIM + ONES_ROWS, TQ_MLA), F32)],
        compiler_params=_params(2),
        name="mla_attn",
    )(q, k, v, wgu_all, wd_all)


def _head_block(nope, rope):
    split = ROPE_PARTNER - ROPE_HALF
    pad = jnp.zeros(nope.shape[:-1] + (LANES - NOPE_DIM - ROPE_DIM,), nope.dtype)
    w = jnp.concatenate([rope[..., :ROPE_HALF], nope[..., :split],
                         rope[..., ROPE_HALF:], nope[..., split:], pad], axis=-1)
    return w.reshape(w.shape[:-2] + (w.shape[-2] * LANES,))


def kernel(x, positions, norm_mix, norm_ffn, a_w_qkv, a_sink, a_w_o, b_w_in, b_g_q, b_g_kv,
           b_w_uq, b_w_ukv, b_w_o, ffn_w_gu, ffn_w_down, final_norm):
    B, S, D = x.shape
    T = B * S
    n_a = a_w_qkv.shape[0]
    n_b = b_w_in.shape[0]
    h = x.reshape(T, D)
    pos_col3 = positions.reshape(B, S, 1)
    pos_blk3 = positions.reshape(B, S // BLK, BLK)

    inv_freq = ROPE_THETA ** (-jnp.arange(ROPE_HALF, dtype=F32) * 2.0 / ROPE_DIM)
    pos_dense = jnp.repeat(positions.reshape(T // TOK_PER_ROW, TOK_PER_ROW), ROPE_HALF, axis=1)
    ct, sg = _rope_tables(pos_dense, jnp.tile(inv_freq, TOK_PER_ROW).reshape(1, LANES))
    ct, sg = (t.reshape(B, S, LANES) for t in (ct, sg))

    g_mix = norm_mix.reshape(DEPTH, 1, D)
    g_ffn = norm_ffn.reshape(DEPTH, 1, D)
    gf = final_norm.reshape(1, D)

    nq = A_HEADS * A_HEAD_DIM
    nqk = nq + A_KV_HEADS * A_HEAD_DIM
    a_wqk = a_w_qkv[:, :, :nqk].astype(BF16)
    a_wvt = jnp.swapaxes(a_w_qkv[:, :, nqk:], 1, 2).astype(BF16)
    a_wo = a_w_o.astype(BF16)

    n_lat = Q_LORA + KV_LORA
    w_kr = _head_block(jnp.zeros((n_b, D, 1, NOPE_DIM), F32),
                       b_w_in[:, :, n_lat:].reshape(n_b, D, 1, ROPE_DIM))
    b_win = jnp.concatenate([b_w_in[:, :, :n_lat], w_kr], axis=-1).astype(BF16)
    w_uq = b_w_uq.reshape(n_b, Q_LORA, B_HEADS, NOPE_DIM + ROPE_DIM)
    b_wuq = _head_block(w_uq[..., :NOPE_DIM], w_uq[..., NOPE_DIM:]).astype(BF16)
    w_ukv = b_w_ukv.reshape(n_b, KV_LORA, B_HEADS, NOPE_DIM + V_DIM)
    b_wuk = _head_block(w_ukv[..., :NOPE_DIM],
                        jnp.zeros((n_b, KV_LORA, B_HEADS, ROPE_DIM), F32)).astype(BF16)
    b_wuvt = jnp.swapaxes(w_ukv[..., NOPE_DIM:].reshape(n_b, KV_LORA, B_HEADS * V_DIM),
                          1, 2).astype(BF16)
    b_gq = b_g_q.reshape(n_b, 1, Q_LORA)
    b_gkv = b_g_kv.reshape(n_b, 1, KV_LORA)
    b_wo = b_w_o.astype(BF16)

    for i in range(DEPTH):
        j = i // 2
        if i % 2 == 0:
            q, kcat, vt = _norm_qkv(h, g_mix, i, a_wqk, a_wvt, j)
            attn, w_gu, w_down = _win_attn(q.reshape(B, S, nq), kcat, vt, pos_col3, pos_blk3,
                                           a_sink[j], ffn_w_gu, ffn_w_down, i)
            attn = attn.reshape(T, nq)
            w_o = a_wo
        else:
            qh, kh, v = _mla_prep(h.reshape(B, S, D), g_mix, i, b_win, b_gq, b_gkv,
                                  b_wuq, b_wuk, b_wuvt, j, ct, sg)
            attn, w_gu, w_down = _mla_attn(qh, kh, v, ffn_w_gu, ffn_w_down, i)
            attn = attn.reshape(T, B_HEADS * V_DIM)
            w_o = b_wo
        h = _proj_ffn(h, attn, w_o, j, g_ffn, w_gu, w_down, i, gf,
                      final_norm=(i == DEPTH - 1))
    return h.reshape(B, S, D)
```

```python
import functools
import math

import jax
import jax.numpy as jnp
from jax import lax
from jax.experimental import pallas as pl
from jax.experimental.pallas import tpu as pltpu

F32 = jnp.float32
BF16 = jnp.bfloat16

D_MODEL = 1024
DEPTH = 4
A_HEADS = 16
A_KV_HEADS = 4
A_GROUP = A_HEADS // A_KV_HEADS
A_HEAD_DIM = 64
WINDOW = 128
BLK = 128
B_HEADS = 16
Q_LORA = 384
KV_LORA = 256
NOPE_DIM = 64
ROPE_DIM = 32
V_DIM = 64
ROPE_THETA = 10000.0
D_FF = 2816
EPS = 1e-6

LOG2E = math.log2(math.e)
MASK_DIST = 1e36

LANES = 128
ONES_ROWS = 64
VMEM_LIMIT = 56 * 1024 * 1024

TM_PROJ = 1024
TM_FFN = 1024
TF_FFN = 256
TM_PREP = 512
SUB_PREP = 256

ROPE_HALF = ROPE_DIM // 2
ROPE_PARTNER = LANES // 2
TQ_MLA = 256
KC_MLA = 256
PAIRS_MLA = 4

NT_DIMS = (((1,), (1,)), ((), ()))


def _rms(x, g):
    ms = jnp.mean(x * x, axis=-1, keepdims=True)
    return x * lax.rsqrt(ms + EPS) * g


def _params(n_axes):
    return pltpu.CompilerParams(dimension_semantics=("arbitrary",) * n_axes,
                                vmem_limit_bytes=VMEM_LIMIT)


def _layer_spec(stacked, layer, **kwargs):
    index = (layer,) + (0,) * (stacked.ndim - 1)
    return pl.BlockSpec((None,) + stacked.shape[1:], lambda *_: index, **kwargs)


def _ffn_weight_slabs(wgu_all, wd_all, layer, n_steps, step_of):
    specs_in, specs_out, shapes = [], [], []
    for w_all in (wgu_all, wd_all):
        rows, cols = w_all.shape[1:]
        slab = rows // n_steps
        specs_in.append(pl.BlockSpec((None, slab, cols),
                                     lambda *idx: (layer, step_of(*idx), 0)))
        specs_out.append(pl.BlockSpec((slab, cols), lambda *idx: (step_of(*idx), 0)))
        shapes.append(jax.ShapeDtypeStruct((rows, cols), BF16))
    return specs_in, specs_out, shapes


def _cast_ffn_weight_slab(wgu_ref, wd_ref, wgu_out_ref, wd_out_ref):
    wgu_out_ref[...] = wgu_ref[...].astype(BF16)
    wd_out_ref[...] = wd_ref[...].astype(BF16)


def _norm_qkv_kernel(x_ref, g_ref, wqk_ref, wvt_ref, q_ref, k_ref, vt_ref):
    nq = A_HEADS * A_HEAD_DIM
    tm = x_ref.shape[0]
    xn = _rms(x_ref[...], g_ref[...]).astype(BF16)
    qk = jnp.dot(xn, wqk_ref[...], preferred_element_type=F32)
    q_ref[...] = (qk[:, :nq] * (A_HEAD_DIM ** -0.5 * LOG2E)).astype(BF16)
    low = lax.broadcasted_iota(jnp.int32, (1, LANES), 1) < A_HEAD_DIM
    for kh in range(A_KV_HEADS):
        pair = qk[:, nq + (kh // 2) * LANES:nq + (kh // 2 + 1) * LANES]
        swap = pltpu.roll(pair, A_HEAD_DIM, 1)
        own, other = (pair, swap) if kh % 2 == 0 else (swap, pair)
        k_ref[kh, 0] = jnp.where(low, own, 0.0).astype(BF16)
        k_ref[kh, 1] = jnp.where(low, 0.0, other).astype(BF16)
    vt = lax.dot_general(wvt_ref[...], xn, NT_DIMS, preferred_element_type=F32)
    ones = jnp.ones((ONES_ROWS, BLK), BF16)
    for kh in range(A_KV_HEADS):
        for tb in range(tm // BLK):
            vt_ref[kh, tb, :A_HEAD_DIM, :] = (
                vt[kh * A_HEAD_DIM:(kh + 1) * A_HEAD_DIM, tb * BLK:(tb + 1) * BLK].astype(BF16))
            vt_ref[kh, tb, A_HEAD_DIM:, :] = ones


def _norm_qkv(x, g_all, layer, wqk_all, wvt_all, mixer):
    T, D = x.shape
    nq = A_HEADS * A_HEAD_DIM
    tm = TM_PROJ
    return pl.pallas_call(
        _norm_qkv_kernel,
        grid=(T // tm,),
        in_specs=[pl.BlockSpec((tm, D), lambda i: (i, 0)),
                  _layer_spec(g_all, layer),
                  _layer_spec(wqk_all, mixer),
                  _layer_spec(wvt_all, mixer)],
        out_specs=[pl.BlockSpec((tm, nq), lambda i: (i, 0)),
                   pl.BlockSpec((A_KV_HEADS, 2, tm, LANES), lambda i: (0, 0, i, 0)),
                   pl.BlockSpec((A_KV_HEADS, tm // BLK, A_HEAD_DIM + ONES_ROWS, BLK),
                                lambda i: (0, i, 0, 0))],
        out_shape=[jax.ShapeDtypeStruct((T, nq), BF16),
                   jax.ShapeDtypeStruct((A_KV_HEADS, 2, T, LANES), BF16),
                   jax.ShapeDtypeStruct((A_KV_HEADS, T // BLK, A_HEAD_DIM + ONES_ROWS, BLK), BF16)],
        compiler_params=_params(1),
        name="norm_qkv",
    )(x, g_all, wqk_all, wvt_all)


def _alibi_slope(h):
    return float(2.0 ** (-8.0 * (h + 1) / A_HEADS))


def _win_attn_kernel(sink_ref, q_ref, k_ref, vt_ref, pc_ref, pr_ref, wgu_ref, wd_ref,
                     o_ref, wgu_out_ref, wd_out_ref, s0_sc, m0_sc, s1_sc, m1_sc, *, n_blocks):
    _cast_ffn_weight_slab(wgu_ref, wd_ref, wgu_out_ref, wd_out_ref)
    nkey = 3 * BLK
    r = lax.broadcasted_iota(jnp.int32, (nkey, BLK), 0)
    c = lax.broadcasted_iota(jnp.int32, (nkey, BLK), 1)
    rc = r - c

    def window(blk):
        if isinstance(blk, int):
            return min(max(blk - 1, 0), n_blocks - 3)
        return jnp.clip(blk - 1, 0, n_blocks - 3)

    def row_start(blk):
        return blk * BLK if isinstance(blk, int) else pl.multiple_of(blk * BLK, BLK)

    def masked_distance(blk):
        wb = window(blk)
        valid = jnp.abs(rc + (wb - blk) * BLK) <= WINDOW
        kpos = pc_ref[pl.ds(row_start(wb), nkey), :]
        qpos = pr_ref[pl.ds(blk, 1), :]
        return jnp.where(valid, jnp.abs(kpos - qpos).astype(F32), MASK_DIST)

    def scores(blk, kh, dm, s_sc, m_sc):
        r0 = row_start(blk)
        w0 = row_start(window(blk))
        col = kh * A_GROUP * A_HEAD_DIM
        qst = jnp.concatenate([q_ref[pl.ds(r0, BLK), col:col + LANES],
                               q_ref[pl.ds(r0, BLK), col + LANES:col + 2 * LANES]], axis=0)
        for part in range(2):
            st = lax.dot_general(k_ref[kh, part, pl.ds(w0, nkey), :], qst, NT_DIMS,
                                 preferred_element_type=F32)
            for half in range(2):
                h = kh * A_GROUP + 2 * half + part
                s = st[:, half * BLK:(half + 1) * BLK] - (_alibi_slope(h) * LOG2E) * dm
                s_sc[h] = s
                m = jnp.maximum(jnp.max(s, axis=0, keepdims=True), sink_ref[h] * LOG2E)
                m_sc[h] = jnp.broadcast_to(m, (8, BLK))

    def context(blk, kh, s_sc, m_sc):
        r0 = row_start(blk)
        wb = window(blk)
        col = kh * A_GROUP * A_HEAD_DIM
        vt = jnp.concatenate([vt_ref[kh, wb + t] for t in range(3)], axis=1)
        normed = [None] * A_GROUP
        for part in range(2):
            heads = [kh * A_GROUP + 2 * half + part for half in range(2)]
            pt = jnp.concatenate(
                [jnp.exp2(s_sc[h] - m_sc[h][0:1, :]).astype(BF16) for h in heads], axis=1)
            ot = jnp.dot(vt, pt, preferred_element_type=F32)
            for half, h in enumerate(heads):
                cols = slice(half * BLK, (half + 1) * BLK)
                l = ot[A_HEAD_DIM:A_HEAD_DIM + 1, cols] + jnp.exp2(
                    sink_ref[h] * LOG2E - m_sc[h][0:1, :])
                normed[2 * half + part] = ot[:A_HEAD_DIM, cols] / l
        for half in range(A_GROUP // 2):
            pair_t = jnp.concatenate([normed[2 * half], normed[2 * half + 1]], axis=0)
            o_ref[pl.ds(r0, BLK), col + half * LANES:col + (half + 1) * LANES] = (
                pair_t.T.astype(BF16))

    def stage(ctx, sc, s_old, m_old, s_new, m_new):
        dm = None if sc is None else masked_distance(sc)
        for kh in range(A_KV_HEADS):
            if ctx is not None:
                context(ctx, kh, s_old, m_old)
            if sc is not None:
                scores(sc, kh, dm, s_new, m_new)

    stage(None, 0, None, None, s0_sc, m0_sc)

    def body(j, carry):
        stage(2 * j, 2 * j + 1, s0_sc, m0_sc, s1_sc, m1_sc)
        stage(2 * j + 1, 2 * j + 2, s1_sc, m1_sc, s0_sc, m0_sc)
        return carry

    lax.fori_loop(0, n_blocks // 2 - 1, body, 0)
    stage(n_blocks - 2, n_blocks - 1, s0_sc, m0_sc, s1_sc, m1_sc)
    stage(n_blocks - 1, None, s1_sc, m1_sc, None, None)


def _win_attn(q, kcat, vt, pos_col, pos_blk, sink, wgu_all, wd_all, layer):
    B, S, nq = q.shape
    nb = S // BLK
    s_buf = pltpu.VMEM((A_HEADS, 3 * BLK, BLK), F32)
    m_buf = pltpu.VMEM((A_HEADS, 8, BLK), F32)
    slab_in, slab_out, slab_shapes = _ffn_weight_slabs(wgu_all, wd_all, layer, B,
                                                       lambda b: b)
    return pl.pallas_call(
        functools.partial(_win_attn_kernel, n_blocks=nb),
        grid=(B,),
        in_specs=[pl.BlockSpec(memory_space=pltpu.SMEM),
                  pl.BlockSpec((None, S, nq), lambda b: (b, 0, 0)),
                  pl.BlockSpec((A_KV_HEADS, 2, S, LANES), lambda b: (0, 0, b, 0)),
                  pl.BlockSpec((A_KV_HEADS, nb, A_HEAD_DIM + ONES_ROWS, BLK), lambda b: (0, b, 0, 0)),
                  pl.BlockSpec((None, S, 1), lambda b: (b, 0, 0)),
                  pl.BlockSpec((None, nb, BLK), lambda b: (b, 0, 0))] + slab_in,
        out_specs=[pl.BlockSpec((None, S, nq), lambda b: (b, 0, 0))] + slab_out,
        out_shape=[jax.ShapeDtypeStruct((B, S, nq), BF16)] + slab_shapes,
        scratch_shapes=[s_buf, m_buf, s_buf, m_buf],
        compiler_params=_params(1),
        name="win_attn",
    )(sink, q, kcat, vt, pos_col, pos_blk, wgu_all, wd_all)


def _proj_ffn_kernel(h_ref, a_ref, wo_ref, g_ref, wgu_ref, wd_ref, gf_ref, out_ref, act_sc,
                     *, final_norm):
    h1 = h_ref[...] + jnp.dot(a_ref[...], wo_ref[...], preferred_element_type=F32)
    xn = _rms(h1, g_ref[...]).astype(BF16)
    for c in range(D_FF // TF_FFN):
        lo = c * TF_FFN
        gate = jnp.dot(xn, wgu_ref[:, lo:lo + TF_FFN], preferred_element_type=F32)
        up = jnp.dot(xn, wgu_ref[:, D_FF + lo:D_FF + lo + TF_FFN], preferred_element_type=F32)
        act_sc[:, lo:lo + TF_FFN] = (gate * jax.nn.sigmoid(gate) * up).astype(BF16)
    y = h1 + jnp.dot(act_sc[...], wd_ref[...], preferred_element_type=F32)
    if final_norm:
        y = _rms(y, gf_ref[...])
    out_ref[...] = y


def _proj_ffn(h, a, wo_all, mixer, g_all, wgu, wd, layer, gf, final_norm):
    T, D = h.shape
    tm = TM_FFN
    row = lambda i: (i, 0)
    const = lambda i: (0, 0)
    resident = pl.Buffered(1)
    return pl.pallas_call(
        functools.partial(_proj_ffn_kernel, final_norm=final_norm),
        grid=(T // tm,),
        in_specs=[pl.BlockSpec((tm, D), row),
                  pl.BlockSpec((tm, a.shape[1]), row),
                  _layer_spec(wo_all, mixer, pipeline_mode=resident),
                  _layer_spec(g_all, layer),
                  pl.BlockSpec(wgu.shape, const, pipeline_mode=resident),
                  pl.BlockSpec(wd.shape, const, pipeline_mode=resident),
                  pl.BlockSpec((1, D), const)],
        out_specs=pl.BlockSpec((tm, D), row),
        out_shape=jax.ShapeDtypeStruct((T, D), F32),
        scratch_shapes=[pltpu.VMEM((tm, D_FF), BF16)],
        compiler_params=_params(1),
        name="proj_ffn",
    )(h, a, wo_all, g_all, wgu, wd, gf)


TOK_PER_ROW = LANES // ROPE_HALF


def _rope_table_kernel(pos_ref, inv_ref, ct_ref, sg_ref):
    ang = pos_ref[...].astype(F32) * inv_ref[...]
    cos = jnp.cos(ang)
    sin = jnp.sin(ang)
    rows = ang.shape[0]
    lane = lax.broadcasted_iota(jnp.int32, ang.shape, 1)
    first = lane < ROPE_HALF
    second = (lane >= ROPE_PARTNER) & (lane < ROPE_PARTNER + ROPE_HALF)
    for g in range(TOK_PER_ROW):
        to_first = (LANES - g * ROPE_HALF) % LANES
        to_second = (to_first + ROPE_PARTNER) % LANES
        c1, c2 = (cos if s == 0 else pltpu.roll(cos, s, 1) for s in (to_first, to_second))
        s1, s2 = (sin if s == 0 else pltpu.roll(sin, s, 1) for s in (to_first, to_second))
        tokens = pl.ds(g, rows, stride=TOK_PER_ROW)
        ct_ref[tokens, :] = jnp.where(first, c1, jnp.where(second, c2, 1.0))
        sg_ref[tokens, :] = jnp.where(first, -s1, jnp.where(second, s2, 0.0))


def _rope_tables(pos_dense, inv_dense):
    T = pos_dense.shape[0] * TOK_PER_ROW
    tm = 2048
    tab = jax.ShapeDtypeStruct((T, LANES), F32)
    return pl.pallas_call(
        _rope_table_kernel,
        grid=(T // tm,),
        in_specs=[pl.BlockSpec((tm // TOK_PER_ROW, LANES), lambda i: (i, 0)),
                  pl.BlockSpec((1, LANES), lambda i: (0, 0))],
        out_specs=[pl.BlockSpec((tm, LANES), lambda i: (i, 0))] * 2,
        out_shape=[tab, tab],
        compiler_params=_params(1),
        name="rope_tables",
    )(pos_dense, inv_dense)


def _mla_prep_kernel(x_ref, g_ref, win_ref, gq_ref, gkv_ref, wuq_ref, wuk_ref, wuvt_ref,
                     ct_ref, sg_ref, q_ref, k_ref, vt_ref):
    scale = (NOPE_DIM + ROPE_DIM) ** -0.5 * LOG2E
    for part in range(TM_PREP // SUB_PREP):
        rows = slice(part * SUB_PREP, (part + 1) * SUB_PREP)
        xn = _rms(x_ref[rows, :], g_ref[...]).astype(BF16)
        lat = jnp.dot(xn, win_ref[...], preferred_element_type=F32)
        cq = _rms(lat[:, :Q_LORA], gq_ref[...]).astype(BF16)
        ckv = _rms(lat[:, Q_LORA:Q_LORA + KV_LORA], gkv_ref[...]).astype(BF16)
        ct = ct_ref[rows, :]
        sg = sg_ref[rows, :]
        k_rope = lat[:, Q_LORA + KV_LORA:]
        k_rope = k_rope * ct + pltpu.roll(k_rope, ROPE_PARTNER, 1) * sg
        ctq = ct * scale
        sgq = sg * scale
        q = jnp.dot(cq, wuq_ref[...], preferred_element_type=F32)
        kn = jnp.dot(ckv, wuk_ref[...], preferred_element_type=F32)
        vt = lax.dot_general(wuvt_ref[...], ckv, NT_DIMS, preferred_element_type=F32)
        ones = jnp.ones((ONES_ROWS, SUB_PREP), BF16)
        for h in range(B_HEADS):
            sl = slice(h * LANES, (h + 1) * LANES)
            qh = q[:, sl]
            q_ref[h, rows, :] = (qh * ctq + pltpu.roll(qh, ROPE_PARTNER, 1) * sgq).astype(BF16)
            k_ref[h, rows, :] = (kn[:, sl] + k_rope).astype(BF16)
            vt_ref[h, :V_DIM, rows] = vt[h * V_DIM:(h + 1) * V_DIM, :].astype(BF16)
            vt_ref[h, V_DIM:, rows] = ones


def _mla_prep(x, g_all, layer, win, gq, gkv, wuq, wuk, wuvt, mixer, ct, sg):
    B, S, D = x.shape
    tm = TM_PREP
    row = lambda b, i: (b, i, 0)
    head = lambda b, i: (b, 0, i, 0)
    head_t = lambda b, i: (b, 0, 0, i)
    qk_shape = jax.ShapeDtypeStruct((B, B_HEADS, S, LANES), BF16)
    return pl.pallas_call(
        _mla_prep_kernel,
        grid=(B, S // tm),
        in_specs=[pl.BlockSpec((None, tm, D), row),
                  _layer_spec(g_all, layer),
                  _layer_spec(win, mixer),
                  _layer_spec(gq, mixer),
                  _layer_spec(gkv, mixer),
                  _layer_spec(wuq, mixer),
                  _layer_spec(wuk, mixer),
                  _layer_spec(wuvt, mixer),
                  pl.BlockSpec((None, tm, LANES), row),
                  pl.BlockSpec((None, tm, LANES), row)],
        out_specs=[pl.BlockSpec((None, B_HEADS, tm, LANES), head),
                   pl.BlockSpec((None, B_HEADS, tm, LANES), head),
                   pl.BlockSpec((None, B_HEADS, V_DIM + ONES_ROWS, tm), head_t)],
        out_shape=[qk_shape, qk_shape,
                   jax.ShapeDtypeStruct((B, B_HEADS, V_DIM + ONES_ROWS, S), BF16)],
        compiler_params=_params(2),
        name="mla_prep",
    )(x, g_all, win, gq, gkv, wuq, wuk, wuvt, ct, sg)


def _mla_attn_kernel(q_ref, k_ref, vt_ref, wgu_ref, wd_ref, o_ref, wgu_out_ref, wd_out_ref,
                     s0_sc, m0_sc, s1_sc, m1_sc, s2_sc, m2_sc, acc_sc):
    _cast_ffn_weight_slab(wgu_ref, wd_ref, wgu_out_ref, wd_out_ref)
    S = q_ref.shape[1]
    n = S // TQ_MLA

    def row_start(blk):
        if isinstance(blk, int):
            return blk * TQ_MLA
        return pl.multiple_of(blk * TQ_MLA, TQ_MLA)

    def finish(item):
        hp, blk = item
        outs = [acc_sc[e][:V_DIM] / acc_sc[e][V_DIM:V_DIM + 1] for e in range(2)]
        o_ref[pl.ds(row_start(blk), TQ_MLA), hp * LANES:(hp + 1) * LANES] = (
            jnp.concatenate(outs, axis=0).T.astype(BF16))

    def stage(fin, ctx, sc, s_old, m_old, s_new, m_new):
        if fin is not None:
            finish(fin)
        for e in range(2):
            if sc is not None:
                head_sc = 2 * sc[0] + e
                q = q_ref[head_sc, pl.ds(row_start(sc[1]), TQ_MLA), :]
                m_run = None
            if ctx is not None:
                head_ctx = 2 * ctx[0] + e
                m = m_old[e][0:1, :]
                acc = None
            for c in range(S // KC_MLA):
                keys = slice(c * KC_MLA, (c + 1) * KC_MLA)
                if ctx is not None:
                    pt = jnp.exp2(s_old[e, keys, :] - m).astype(BF16)
                    part = jnp.dot(vt_ref[head_ctx, :, keys], pt, preferred_element_type=F32)
                    acc = part if acc is None else acc + part
                if sc is not None:
                    st = lax.dot_general(k_ref[head_sc, keys, :], q, NT_DIMS,
                                         preferred_element_type=F32)
                    s_new[e, keys, :] = st
                    m_chunk = jnp.max(st, axis=0, keepdims=True)
                    m_run = m_chunk if m_run is None else jnp.maximum(m_run, m_chunk)
            if sc is not None:
                m_new[e] = jnp.broadcast_to(m_run, (8, TQ_MLA))
            if ctx is not None:
                acc_sc[e] = acc

    n_items = PAIRS_MLA * n
    scratch = ((s0_sc, m0_sc), (s1_sc, m1_sc), (s2_sc, m2_sc))
    depth = len(scratch)

    def item(t):
        return (t // n, t % n) if 0 <= t < n_items else None

    def static_stage(k):
        stage(item(k - 2), item(k - 1), item(k),
              *scratch[(k - 1) % depth], *scratch[k % depth])

    for hp in range(PAIRS_MLA):
        static_stage(hp * n)
        static_stage(hp * n + 1)

        def body(j, carry, hp=hp):
            for t in range(depth):
                k0 = hp * n + 2 + t
                blk = depth * j + t
                stage((hp, blk), (hp, blk + 1), (hp, blk + 2),
                      *scratch[(k0 - 1) % depth], *scratch[k0 % depth])
            return carry

        lax.fori_loop(0, (n - 2) // depth, body, 0)
    static_stage(n_items)
    static_stage(n_items + 1)


def _mla_attn(q, k, v, wgu_all, wd_all, layer):
    B, H, S, _ = q.shape
    heads = 2 * PAIRS_MLA
    groups = H // heads
    pair = lambda b, g: (b, g, 0, 0)
    s_buf = pltpu.VMEM((2, S, TQ_MLA), F32)
    m_buf = pltpu.VMEM((2, 8, TQ_MLA), F32)
    slab_in, slab_out, slab_shapes = _ffn_weight_slabs(wgu_all, wd_all, layer, B * groups,
                                                       lambda b, g: b * groups + g)
    return pl.pallas_call(
        _mla_attn_kernel,
        grid=(B, groups),
        in_specs=[pl.BlockSpec((None, heads, S, LANES), pair),
                  pl.BlockSpec((None, heads, S, LANES), pair),
                  pl.BlockSpec((None, heads, V_DIM + ONES_ROWS, S), pair)] + slab_in,
        out_specs=[pl.BlockSpec((None, S, PAIRS_MLA * LANES), lambda b, g: (b, 0, g))] + slab_out,
        out_shape=[jax.ShapeDtypeStruct((B, S, H * V_DIM), BF16)] + slab_shapes,
        scratch_shapes=[s_buf, m_buf, s_buf, m_buf, s_buf, m_buf,
                        pltpu.VMEM((2, V_DIM + ONES_ROWS, TQ_MLA), F32)],
        compiler_params=_params(2),
        name="mla_attn",
    )(q, k, v, wgu_all, wd_all)


def _head_block(nope, rope):
    split = ROPE_PARTNER - ROPE_HALF
    pad = jnp.zeros(nope.shape[:-1] + (LANES - NOPE_DIM - ROPE_DIM,), nope.dtype)
    w = jnp.concatenate([rope[..., :ROPE_HALF], nope[..., :split],
                         rope[..., ROPE_HALF:], nope[..., split:], pad], axis=-1)
    return w.reshape(w.shape[:-2] + (w.shape[-2] * LANES,))


def kernel(x, positions, norm_mix, norm_ffn, a_w_qkv, a_sink, a_w_o, b_w_in, b_g_q, b_g_kv,
           b_w_uq, b_w_ukv, b_w_o, ffn_w_gu, ffn_w_down, final_norm):
    B, S, D = x.shape
    T = B * S
    n_a = a_w_qkv.shape[0]
    n_b = b_w_in.shape[0]
    h = x.reshape(T, D)
    pos_col3 = positions.reshape(B, S, 1)
    pos_blk3 = positions.reshape(B, S // BLK, BLK)

    inv_freq = ROPE_THETA ** (-jnp.arange(ROPE_HALF, dtype=F32) * 2.0 / ROPE_DIM)
    pos_dense = jnp.repeat(positions.reshape(T // TOK_PER_ROW, TOK_PER_ROW), ROPE_HALF, axis=1)
    ct, sg = _rope_tables(pos_dense, jnp.tile(inv_freq, TOK_PER_ROW).reshape(1, LANES))
    ct, sg = (t.reshape(B, S, LANES) for t in (ct, sg))

    g_mix = norm_mix.reshape(DEPTH, 1, D)
    g_ffn = norm_ffn.reshape(DEPTH, 1, D)
    gf = final_norm.reshape(1, D)

    nq = A_HEADS * A_HEAD_DIM
    nqk = nq + A_KV_HEADS * A_HEAD_DIM
    a_wqk = a_w_qkv[:, :, :nqk].astype(BF16)
    a_wvt = jnp.swapaxes(a_w_qkv[:, :, nqk:], 1, 2).astype(BF16)
    a_wo = a_w_o.astype(BF16)

    n_lat = Q_LORA + KV_LORA
    w_kr = _head_block(jnp.zeros((n_b, D, 1, NOPE_DIM), F32),
                       b_w_in[:, :, n_lat:].reshape(n_b, D, 1, ROPE_DIM))
    b_win = jnp.concatenate([b_w_in[:, :, :n_lat], w_kr], axis=-1).astype(BF16)
    w_uq = b_w_uq.reshape(n_b, Q_LORA, B_HEADS, NOPE_DIM + ROPE_DIM)
    b_wuq = _head_block(w_uq[..., :NOPE_DIM], w_uq[..., NOPE_DIM:]).astype(BF16)
    w_ukv = b_w_ukv.reshape(n_b, KV_LORA, B_HEADS, NOPE_DIM + V_DIM)
    b_wuk = _head_block(w_ukv[..., :NOPE_DIM],
                        jnp.zeros((n_b, KV_LORA, B_HEADS, ROPE_DIM), F32)).astype(BF16)
    b_wuvt = jnp.swapaxes(w_ukv[..., NOPE_DIM:].reshape(n_b, KV_LORA, B_HEADS * V_DIM),
                          1, 2).astype(BF16)
    b_gq = b_g_q.reshape(n_b, 1, Q_LORA)
    b_gkv = b_g_kv.reshape(n_b, 1, KV_LORA)
    b_wo = b_w_o.astype(BF16)

    for i in range(DEPTH):
        j = i // 2
        if i % 2 == 0:
            q, kcat, vt = _norm_qkv(h, g_mix, i, a_wqk, a_wvt, j)
            attn, w_gu, w_down = _win_attn(q.reshape(B, S, nq), kcat, vt, pos_col3, pos_blk3,
                                           a_sink[j], ffn_w_gu, ffn_w_down, i)
            attn = attn.reshape(T, nq)
            w_o = a_wo
        else:
            qh, kh, v = _mla_prep(h.reshape(B, S, D), g_mix, i, b_win, b_gq, b_gkv,
                                  b_wuq, b_wuk, b_wuvt, j, ct, sg)
            attn, w_gu, w_down = _mla_attn(qh, kh, v, ffn_w_gu, ffn_w_down, i)
            attn = attn.reshape(T, B_HEADS * V_DIM)
            w_o = b_wo
        h = _proj_ffn(h, attn, w_o, j, g_ffn, w_gu, w_down, i, gf,
                      final_norm=(i == DEPTH - 1))
    return h.reshape(B, S, D)
```

```python
import functools
import math

import jax
import jax.numpy as jnp
from jax import lax
from jax.experimental import pallas as pl
from jax.experimental.pallas import tpu as pltpu

F32 = jnp.float32
BF16 = jnp.bfloat16

D_MODEL = 1024
DEPTH = 4
A_HEADS = 16
A_KV_HEADS = 4
A_GROUP = A_HEADS // A_KV_HEADS
A_HEAD_DIM = 64
WINDOW = 128
BLK = 128
B_HEADS = 16
Q_LORA = 384
KV_LORA = 256
NOPE_DIM = 64
ROPE_DIM = 32
V_DIM = 64
ROPE_THETA = 10000.0
D_FF = 2816
EPS = 1e-6

LOG2E = math.log2(math.e)
MASK_DIST = 1e36

LANES = 128
ONES_ROWS = 64
VMEM_LIMIT = 56 * 1024 * 1024

TM_PROJ = 1024
TM_FFN = 1024
TF_FFN = 256
TM_PREP = 512
SUB_PREP = 256

ROPE_HALF = ROPE_DIM // 2
ROPE_PARTNER = LANES // 2
TQ_MLA = 256
KC_MLA = 256
PAIRS_MLA = 4

NT_DIMS = (((1,), (1,)), ((), ()))


def _rms(x, g):
    ms = jnp.mean(x * x, axis=-1, keepdims=True)
    return x * lax.rsqrt(ms + EPS) * g


def _params(n_axes):
    return pltpu.CompilerParams(dimension_semantics=("arbitrary",) * n_axes,
                                vmem_limit_bytes=VMEM_LIMIT)


def _layer_spec(stacked, layer, **kwargs):
    index = (layer,) + (0,) * (stacked.ndim - 1)
    return pl.BlockSpec((None,) + stacked.shape[1:], lambda *_: index, **kwargs)


def _ffn_weight_slabs(wgu_all, wd_all, layer, n_steps, step_of):
    specs_in, specs_out, shapes = [], [], []
    for w_all in (wgu_all, wd_all):
        rows, cols = w_all.shape[1:]
        slab = rows // n_steps
        specs_in.append(pl.BlockSpec((None, slab, cols),
                                     lambda *idx: (layer, step_of(*idx), 0)))
        specs_out.append(pl.BlockSpec((slab, cols), lambda *idx: (step_of(*idx), 0)))
        shapes.append(jax.ShapeDtypeStruct((rows, cols), BF16))
    return specs_in, specs_out, shapes


def _cast_ffn_weight_slab(wgu_ref, wd_ref, wgu_out_ref, wd_out_ref):
    wgu_out_ref[...] = wgu_ref[...].astype(BF16)
    wd_out_ref[...] = wd_ref[...].astype(BF16)


def _norm_qkv_kernel(x_ref, g_ref, wqk_ref, wvt_ref, q_ref, k_ref, vt_ref):
    nq = A_HEADS * A_HEAD_DIM
    tm = x_ref.shape[0]
    xn = _rms(x_ref[...], g_ref[...]).astype(BF16)
    qk = jnp.dot(xn, wqk_ref[...], preferred_element_type=F32)
    q_ref[...] = (qk[:, :nq] * (A_HEAD_DIM ** -0.5 * LOG2E)).astype(BF16)
    low = lax.broadcasted_iota(jnp.int32, (1, LANES), 1) < A_HEAD_DIM
    for kh in range(A_KV_HEADS):
        pair = qk[:, nq + (kh // 2) * LANES:nq + (kh // 2 + 1) * LANES]
        swap = pltpu.roll(pair, A_HEAD_DIM, 1)
        own, other = (pair, swap) if kh % 2 == 0 else (swap, pair)
        k_ref[kh, 0] = jnp.where(low, own, 0.0).astype(BF16)
        k_ref[kh, 1] = jnp.where(low, 0.0, other).astype(BF16)
    vt = lax.dot_general(wvt_ref[...], xn, NT_DIMS, preferred_element_type=F32)
    ones = jnp.ones((ONES_ROWS, BLK), BF16)
    for kh in range(A_KV_HEADS):
        for tb in range(tm // BLK):
            vt_ref[kh, tb, :A_HEAD_DIM, :] = (
                vt[kh * A_HEAD_DIM:(kh + 1) * A_HEAD_DIM, tb * BLK:(tb + 1) * BLK].astype(BF16))
            vt_ref[kh, tb, A_HEAD_DIM:, :] = ones


def _norm_qkv(x, g_all, layer, wqk_all, wvt_all, mixer):
    T, D = x.shape
    nq = A_HEADS * A_HEAD_DIM
    tm = TM_PROJ
    return pl.pallas_call(
        _norm_qkv_kernel,
        grid=(T // tm,),
        in_specs=[pl.BlockSpec((tm, D), lambda i: (i, 0)),
                  _layer_spec(g_all, layer),
                  _layer_spec(wqk_all, mixer),
                  _layer_spec(wvt_all, mixer)],
        out_specs=[pl.BlockSpec((tm, nq), lambda i: (i, 0)),
                   pl.BlockSpec((A_KV_HEADS, 2, tm, LANES), lambda i: (0, 0, i, 0)),
                   pl.BlockSpec((A_KV_HEADS, tm // BLK, A_HEAD_DIM + ONES_ROWS, BLK),
                                lambda i: (0, i, 0, 0))],
        out_shape=[jax.ShapeDtypeStruct((T, nq), BF16),
                   jax.ShapeDtypeStruct((A_KV_HEADS, 2, T, LANES), BF16),
                   jax.ShapeDtypeStruct((A_KV_HEADS, T // BLK, A_HEAD_DIM + ONES_ROWS, BLK), BF16)],
        compiler_params=_params(1),
        name="norm_qkv",
    )(x, g_all, wqk_all, wvt_all)


def _alibi_slope(h):
    return float(2.0 ** (-8.0 * (h + 1) / A_HEADS))


def _win_attn_kernel(sink_ref, q_ref, k_ref, vt_ref, pc_ref, pr_ref, wgu_ref, wd_ref,
                     o_ref, wgu_out_ref, wd_out_ref, s0_sc, m0_sc, s1_sc, m1_sc, *, n_blocks):
    _cast_ffn_weight_slab(wgu_ref, wd_ref, wgu_out_ref, wd_out_ref)
    nkey = 3 * BLK
    r = lax.broadcasted_iota(jnp.int32, (nkey, BLK), 0)
    c = lax.broadcasted_iota(jnp.int32, (nkey, BLK), 1)
    rc = r - c

    def window(blk):
        if isinstance(blk, int):
            return min(max(blk - 1, 0), n_blocks - 3)
        return jnp.clip(blk - 1, 0, n_blocks - 3)

    def row_start(blk):
        return blk * BLK if isinstance(blk, int) else pl.multiple_of(blk * BLK, BLK)

    def masked_distance(blk):
        wb = window(blk)
        valid = jnp.abs(rc + (wb - blk) * BLK) <= WINDOW
        kpos = pc_ref[pl.ds(row_start(wb), nkey), :]
        qpos = pr_ref[pl.ds(blk, 1), :]
        return jnp.where(valid, jnp.abs(kpos - qpos).astype(F32), MASK_DIST)

    def scores(blk, kh, dm, s_sc, m_sc):
        r0 = row_start(blk)
        w0 = row_start(window(blk))
        col = kh * A_GROUP * A_HEAD_DIM
        qst = jnp.concatenate([q_ref[pl.ds(r0, BLK), col:col + LANES],
                               q_ref[pl.ds(r0, BLK), col + LANES:col + 2 * LANES]], axis=0)
        for part in range(2):
            st = lax.dot_general(k_ref[kh, part, pl.ds(w0, nkey), :], qst, NT_DIMS,
                                 preferred_element_type=F32)
            for half in range(2):
                h = kh * A_GROUP + 2 * half + part
                s = st[:, half * BLK:(half + 1) * BLK] - (_alibi_slope(h) * LOG2E) * dm
                s_sc[h] = s
                m = jnp.maximum(jnp.max(s, axis=0, keepdims=True), sink_ref[h] * LOG2E)
                m_sc[h] = jnp.broadcast_to(m, (8, BLK))

    def context(blk, kh, s_sc, m_sc):
        r0 = row_start(blk)
        wb = window(blk)
        col = kh * A_GROUP * A_HEAD_DIM
        vt = jnp.concatenate([vt_ref[kh, wb + t] for t in range(3)], axis=1)
        normed = [None] * A_GROUP
        for part in range(2):
            heads = [kh * A_GROUP + 2 * half + part for half in range(2)]
            pt = jnp.concatenate(
                [jnp.exp2(s_sc[h] - m_sc[h][0:1, :]).astype(BF16) for h in heads], axis=1)
            ot = jnp.dot(vt, pt, preferred_element_type=F32)
            for half, h in enumerate(heads):
                cols = slice(half * BLK, (half + 1) * BLK)
                l = ot[A_HEAD_DIM:A_HEAD_DIM + 1, cols] + jnp.exp2(
                    sink_ref[h] * LOG2E - m_sc[h][0:1, :])
                normed[2 * half + part] = ot[:A_HEAD_DIM, cols] / l
        for half in range(A_GROUP // 2):
            pair_t = jnp.concatenate([normed[2 * half], normed[2 * half + 1]], axis=0)
            o_ref[pl.ds(r0, BLK), col + half * LANES:col + (half + 1) * LANES] = (
                pair_t.T.astype(BF16))

    def stage(ctx, sc, s_old, m_old, s_new, m_new):
        dm = None if sc is None else masked_distance(sc)
        for kh in range(A_KV_HEADS):
            if ctx is not None:
                context(ctx, kh, s_old, m_old)
            if sc is not None:
                scores(sc, kh, dm, s_new, m_new)

    stage(None, 0, None, None, s0_sc, m0_sc)

    def body(j, carry):
        stage(2 * j, 2 * j + 1, s0_sc, m0_sc, s1_sc, m1_sc)
        stage(2 * j + 1, 2 * j + 2, s1_sc, m1_sc, s0_sc, m0_sc)
        return carry

    lax.fori_loop(0, n_blocks // 2 - 1, body, 0)
    stage(n_blocks - 2, n_blocks - 1, s0_sc, m0_sc, s1_sc, m1_sc)
    stage(n_blocks - 1, None, s1_sc, m1_sc, None, None)


def _win_attn(q, kcat, vt, pos_col, pos_blk, sink, wgu_all, wd_all, layer):
    B, S, nq = q.shape
    nb = S // BLK
    s_buf = pltpu.VMEM((A_HEADS, 3 * BLK, BLK), F32)
    m_buf = pltpu.VMEM((A_HEADS, 8, BLK), F32)
    slab_in, slab_out, slab_shapes = _ffn_weight_slabs(wgu_all, wd_all, layer, B,
                                                       lambda b: b)
    return pl.pallas_call(
        functools.partial(_win_attn_kernel, n_blocks=nb),
        grid=(B,),
        in_specs=[pl.BlockSpec(memory_space=pltpu.SMEM),
                  pl.BlockSpec((None, S, nq), lambda b: (b, 0, 0)),
                  pl.BlockSpec((A_KV_HEADS, 2, S, LANES), lambda b: (0, 0, b, 0)),
                  pl.BlockSpec((A_KV_HEADS, nb, A_HEAD_DIM + ONES_ROWS, BLK), lambda b: (0, b, 0, 0)),
                  pl.BlockSpec((None, S, 1), lambda b: (b, 0, 0)),
                  pl.BlockSpec((None, nb, BLK), lambda b: (b, 0, 0))] + slab_in,
        out_specs=[pl.BlockSpec((None, S, nq), lambda b: (b, 0, 0))] + slab_out,
        out_shape=[jax.ShapeDtypeStruct((B, S, nq), BF16)] + slab_shapes,
        scratch_shapes=[s_buf, m_buf, s_buf, m_buf],
        compiler_params=_params(1),
        name="win_attn",
    )(sink, q, kcat, vt, pos_col, pos_blk, wgu_all, wd_all)


def _proj_ffn_kernel(h_ref, a_ref, wo_ref, g_ref, wgu_ref, wd_ref, gf_ref, out_ref, act_sc,
                     *, final_norm):
    half = h_ref.shape[0] // 2
    rows = [slice(r * half, (r + 1) * half) for r in range(2)]
    h1 = [h_ref[rw, :] + jnp.dot(a_ref[rw, :], wo_ref[...], preferred_element_type=F32)
          for rw in rows]
    xn = [_rms(h1_r, g_ref[...]).astype(BF16) for h1_r in h1]
    for rw, xn_r in zip(rows, xn):
        for c in range(D_FF // TF_FFN):
            lo = c * TF_FFN
            gate = jnp.dot(xn_r, wgu_ref[:, lo:lo + TF_FFN], preferred_element_type=F32)
            up = jnp.dot(xn_r, wgu_ref[:, D_FF + lo:D_FF + lo + TF_FFN],
                         preferred_element_type=F32)
            act_sc[rw, lo:lo + TF_FFN] = (gate * jax.nn.sigmoid(gate) * up).astype(BF16)
    for rw, h1_r in zip(rows, h1):
        y = h1_r + jnp.dot(act_sc[rw, :], wd_ref[...], preferred_element_type=F32)
        if final_norm:
            y = _rms(y, gf_ref[...])
        out_ref[rw, :] = y


def _proj_ffn(h, a, wo_all, mixer, g_all, wgu, wd, layer, gf, final_norm):
    T, D = h.shape
    tm = TM_FFN
    row = lambda i: (i, 0)
    const = lambda i: (0, 0)
    resident = pl.Buffered(1)
    return pl.pallas_call(
        functools.partial(_proj_ffn_kernel, final_norm=final_norm),
        grid=(T // tm,),
        in_specs=[pl.BlockSpec((tm, D), row),
                  pl.BlockSpec((tm, a.shape[1]), row),
                  _layer_spec(wo_all, mixer, pipeline_mode=resident),
                  _layer_spec(g_all, layer),
                  pl.BlockSpec(wgu.shape, const, pipeline_mode=resident),
                  pl.BlockSpec(wd.shape, const, pipeline_mode=resident),
                  pl.BlockSpec((1, D), const)],
        out_specs=pl.BlockSpec((tm, D), row),
        out_shape=jax.ShapeDtypeStruct((T, D), F32),
        scratch_shapes=[pltpu.VMEM((tm, D_FF), BF16)],
        compiler_params=_params(1),
        name="proj_ffn",
    )(h, a, wo_all, g_all, wgu, wd, gf)


TOK_PER_ROW = LANES // ROPE_HALF


def _rope_table_kernel(pos_ref, inv_ref, ct_ref, sg_ref):
    ang = pos_ref[...].astype(F32) * inv_ref[...]
    cos = jnp.cos(ang)
    sin = jnp.sin(ang)
    rows = ang.shape[0]
    lane = lax.broadcasted_iota(jnp.int32, ang.shape, 1)
    first = lane < ROPE_HALF
    second = (lane >= ROPE_PARTNER) & (lane < ROPE_PARTNER + ROPE_HALF)
    for g in range(TOK_PER_ROW):
        to_first = (LANES - g * ROPE_HALF) % LANES
        to_second = (to_first + ROPE_PARTNER) % LANES
        c1, c2 = (cos if s == 0 else pltpu.roll(cos, s, 1) for s in (to_first, to_second))
        s1, s2 = (sin if s == 0 else pltpu.roll(sin, s, 1) for s in (to_first, to_second))
        tokens = pl.ds(g, rows, stride=TOK_PER_ROW)
        ct_ref[tokens, :] = jnp.where(first, c1, jnp.where(second, c2, 1.0))
        sg_ref[tokens, :] = jnp.where(first, -s1, jnp.where(second, s2, 0.0))


def _rope_tables(pos_dense, inv_dense):
    T = pos_dense.shape[0] * TOK_PER_ROW
    tm = 2048
    tab = jax.ShapeDtypeStruct((T, LANES), F32)
    return pl.pallas_call(
        _rope_table_kernel,
        grid=(T // tm,),
        in_specs=[pl.BlockSpec((tm // TOK_PER_ROW, LANES), lambda i: (i, 0)),
                  pl.BlockSpec((1, LANES), lambda i: (0, 0))],
        out_specs=[pl.BlockSpec((tm, LANES), lambda i: (i, 0))] * 2,
        out_shape=[tab, tab],
        compiler_params=_params(1),
        name="rope_tables",
    )(pos_dense, inv_dense)


def _mla_prep_kernel(x_ref, g_ref, win_ref, gq_ref, gkv_ref, wuq_ref, wuk_ref, wuvt_ref,
                     ct_ref, sg_ref, q_ref, k_ref, vt_ref):
    scale = (NOPE_DIM + ROPE_DIM) ** -0.5 * LOG2E
    for part in range(TM_PREP // SUB_PREP):
        rows = slice(part * SUB_PREP, (part + 1) * SUB_PREP)
        xn = _rms(x_ref[rows, :], g_ref[...]).astype(BF16)
        lat = jnp.dot(xn, win_ref[...], preferred_element_type=F32)
        cq = _rms(lat[:, :Q_LORA], gq_ref[...]).astype(BF16)
        ckv = _rms(lat[:, Q_LORA:Q_LORA + KV_LORA], gkv_ref[...]).astype(BF16)
        ct = ct_ref[rows, :]
        sg = sg_ref[rows, :]
        k_rope = lat[:, Q_LORA + KV_LORA:]
        k_rope = k_rope * ct + pltpu.roll(k_rope, ROPE_PARTNER, 1) * sg
        ctq = ct * scale
        sgq = sg * scale
        q = jnp.dot(cq, wuq_ref[...], preferred_element_type=F32)
        kn = jnp.dot(ckv, wuk_ref[...], preferred_element_type=F32)
        vt = lax.dot_general(wuvt_ref[...], ckv, NT_DIMS, preferred_element_type=F32)
        ones = jnp.ones((ONES_ROWS, SUB_PREP), BF16)
        for h in range(B_HEADS):
            sl = slice(h * LANES, (h + 1) * LANES)
            qh = q[:, sl]
            q_ref[h, rows, :] = (qh * ctq + pltpu.roll(qh, ROPE_PARTNER, 1) * sgq).astype(BF16)
            k_ref[h, rows, :] = (kn[:, sl] + k_rope).astype(BF16)
            vt_ref[h, :V_DIM, rows] = vt[h * V_DIM:(h + 1) * V_DIM, :].astype(BF16)
            vt_ref[h, V_DIM:, rows] = ones


def _mla_prep(x, g_all, layer, win, gq, gkv, wuq, wuk, wuvt, mixer, ct, sg):
    B, S, D = x.shape
    tm = TM_PREP
    row = lambda b, i: (b, i, 0)
    head = lambda b, i: (b, 0, i, 0)
    head_t = lambda b, i: (b, 0, 0, i)
    qk_shape = jax.ShapeDtypeStruct((B, B_HEADS, S, LANES), BF16)
    return pl.pallas_call(
        _mla_prep_kernel,
        grid=(B, S // tm),
        in_specs=[pl.BlockSpec((None, tm, D), row),
                  _layer_spec(g_all, layer),
                  _layer_spec(win, mixer),
                  _layer_spec(gq, mixer),
                  _layer_spec(gkv, mixer),
                  _layer_spec(wuq, mixer),
                  _layer_spec(wuk, mixer),
                  _layer_spec(wuvt, mixer),
                  pl.BlockSpec((None, tm, LANES), row),
                  pl.BlockSpec((None, tm, LANES), row)],
        out_specs=[pl.BlockSpec((None, B_HEADS, tm, LANES), head),
                   pl.BlockSpec((None, B_HEADS, tm, LANES), head),
                   pl.BlockSpec((None, B_HEADS, V_DIM + ONES_ROWS, tm), head_t)],
        out_shape=[qk_shape, qk_shape,
                   jax.ShapeDtypeStruct((B, B_HEADS, V_DIM + ONES_ROWS, S), BF16)],
        compiler_params=_params(2),
        name="mla_prep",
    )(x, g_all, win, gq, gkv, wuq, wuk, wuvt, ct, sg)


def _mla_attn_kernel(q_ref, k_ref, vt_ref, wgu_ref, wd_ref, o_ref, wgu_out_ref, wd_out_ref,
                     s0_sc, m0_sc, s1_sc, m1_sc, s2_sc, m2_sc, acc_sc):
    _cast_ffn_weight_slab(wgu_ref, wd_ref, wgu_out_ref, wd_out_ref)
    S = q_ref.shape[1]
    n = S // TQ_MLA

    def row_start(blk):
        if isinstance(blk, int):
            return blk * TQ_MLA
        return pl.multiple_of(blk * TQ_MLA, TQ_MLA)

    def finish(item):
        hp, blk = item
        outs = [acc_sc[e][:V_DIM] / acc_sc[e][V_DIM:V_DIM + 1] for e in range(2)]
        o_ref[pl.ds(row_start(blk), TQ_MLA), hp * LANES:(hp + 1) * LANES] = (
            jnp.concatenate(outs, axis=0).T.astype(BF16))

    def stage(fin, ctx, sc, s_old, m_old, s_new, m_new):
        if fin is not None:
            finish(fin)
        for e in range(2):
            if sc is not None:
                head_sc = 2 * sc[0] + e
                q = q_ref[head_sc, pl.ds(row_start(sc[1]), TQ_MLA), :]
                m_run = None
            if ctx is not None:
                head_ctx = 2 * ctx[0] + e
                m = m_old[e][0:1, :]
                acc = None
            for c in range(S // KC_MLA):
                keys = slice(c * KC_MLA, (c + 1) * KC_MLA)
                if ctx is not None:
                    pt = jnp.exp2(s_old[e, keys, :] - m).astype(BF16)
                    part = jnp.dot(vt_ref[head_ctx, :, keys], pt, preferred_element_type=F32)
                    acc = part if acc is None else acc + part
                if sc is not None:
                    st = lax.dot_general(k_ref[head_sc, keys, :], q, NT_DIMS,
                                         preferred_element_type=F32)
                    s_new[e, keys, :] = st
                    m_chunk = jnp.max(st, axis=0, keepdims=True)
                    m_run = m_chunk if m_run is None else jnp.maximum(m_run, m_chunk)
            if sc is not None:
                m_new[e] = jnp.broadcast_to(m_run, (8, TQ_MLA))
            if ctx is not None:
                acc_sc[e] = acc

    n_items = PAIRS_MLA * n
    scratch = ((s0_sc, m0_sc), (s1_sc, m1_sc), (s2_sc, m2_sc))
    depth = len(scratch)

    def item(t):
        return (t // n, t % n) if 0 <= t < n_items else None

    def static_stage(k):
        stage(item(k - 2), item(k - 1), item(k),
              *scratch[(k - 1) % depth], *scratch[k % depth])

    for hp in range(PAIRS_MLA):
        static_stage(hp * n)
        static_stage(hp * n + 1)

        def body(j, carry, hp=hp):
            for t in range(depth):
                k0 = hp * n + 2 + t
                blk = depth * j + t
                stage((hp, blk), (hp, blk + 1), (hp, blk + 2),
                      *scratch[(k0 - 1) % depth], *scratch[k0 % depth])
            return carry

        lax.fori_loop(0, (n - 2) // depth, body, 0)
    static_stage(n_items)
    static_stage(n_items + 1)


def _mla_attn(q, k, v, wgu_all, wd_all, layer):
    B, H, S, _ = q.shape
    heads = 2 * PAIRS_MLA
    groups = H // heads
    pair = lambda b, g: (b, g, 0, 0)
    s_buf = pltpu.VMEM((2, S, TQ_MLA), F32)
    m_buf = pltpu.VMEM((2, 8, TQ_MLA), F32)
    slab_in, slab_out, slab_shapes = _ffn_weight_slabs(wgu_all, wd_all, layer, B * groups,
                                                       lambda b, g: b * groups + g)
    return pl.pallas_call(
        _mla_attn_kernel,
        grid=(B, groups),
        in_specs=[pl.BlockSpec((None, heads, S, LANES), pair),
                  pl.BlockSpec((None, heads, S, LANES), pair),
                  pl.BlockSpec((None, heads, V_DIM + ONES_ROWS, S), pair)] + slab_in,
        out_specs=[pl.BlockSpec((None, S, PAIRS_MLA * LANES), lambda b, g: (b, 0, g))] + slab_out,
        out_shape=[jax.ShapeDtypeStruct((B, S, H * V_DIM), BF16)] + slab_shapes,
        scratch_shapes=[s_buf, m_buf, s_buf, m_buf, s_buf, m_buf,
                        pltpu.VMEM((2, V_DIM + ONES_ROWS, TQ_MLA), F32)],
        compiler_params=_params(2),
        name="mla_attn",
    )(q, k, v, wgu_all, wd_all)


def _head_block(nope, rope):
    split = ROPE_PARTNER - ROPE_HALF
    pad = jnp.zeros(nope.shape[:-1] + (LANES - NOPE_DIM - ROPE_DIM,), nope.dtype)
    w = jnp.concatenate([rope[..., :ROPE_HALF], nope[..., :split],
                         rope[..., ROPE_HALF:], nope[..., split:], pad], axis=-1)
    return w.reshape(w.shape[:-2] + (w.shape[-2] * LANES,))


def kernel(x, positions, norm_mix, norm_ffn, a_w_qkv, a_sink, a_w_o, b_w_in, b_g_q, b_g_kv,
           b_w_uq, b_w_ukv, b_w_o, ffn_w_gu, ffn_w_down, final_norm):
    B, S, D = x.shape
    T = B * S
    n_a = a_w_qkv.shape[0]
    n_b = b_w_in.shape[0]
    h = x.reshape(T, D)
    pos_col3 = positions.reshape(B, S, 1)
    pos_blk3 = positions.reshape(B, S // BLK, BLK)

    inv_freq = ROPE_THETA ** (-jnp.arange(ROPE_HALF, dtype=F32) * 2.0 / ROPE_DIM)
    pos_dense = jnp.repeat(positions.reshape(T // TOK_PER_ROW, TOK_PER_ROW), ROPE_HALF, axis=1)
    ct, sg = _rope_tables(pos_dense, jnp.tile(inv_freq, TOK_PER_ROW).reshape(1, LANES))
    ct, sg = (t.reshape(B, S, LANES) for t in (ct, sg))

    g_mix = norm_mix.reshape(DEPTH, 1, D)
    g_ffn = norm_ffn.reshape(DEPTH, 1, D)
    gf = final_norm.reshape(1, D)

    nq = A_HEADS * A_HEAD_DIM
    nqk = nq + A_KV_HEADS * A_HEAD_DIM
    a_wqk = a_w_qkv[:, :, :nqk].astype(BF16)
    a_wvt = jnp.swapaxes(a_w_qkv[:, :, nqk:], 1, 2).astype(BF16)
    a_wo = a_w_o.astype(BF16)

    n_lat = Q_LORA + KV_LORA
    w_kr = _head_block(jnp.zeros((n_b, D, 1, NOPE_DIM), F32),
                       b_w_in[:, :, n_lat:].reshape(n_b, D, 1, ROPE_DIM))
    b_win = jnp.concatenate([b_w_in[:, :, :n_lat], w_kr], axis=-1).astype(BF16)
    w_uq = b_w_uq.reshape(n_b, Q_LORA, B_HEADS, NOPE_DIM + ROPE_DIM)
    b_wuq = _head_block(w_uq[..., :NOPE_DIM], w_uq[..., NOPE_DIM:]).astype(BF16)
    w_ukv = b_w_ukv.reshape(n_b, KV_LORA, B_HEADS, NOPE_DIM + V_DIM)
    b_wuk = _head_block(w_ukv[..., :NOPE_DIM],
                        jnp.zeros((n_b, KV_LORA, B_HEADS, ROPE_DIM), F32)).astype(BF16)
    b_wuvt = jnp.swapaxes(w_ukv[..., NOPE_DIM:].reshape(n_b, KV_LORA, B_HEADS * V_DIM),
                          1, 2).astype(BF16)
    b_gq = b_g_q.reshape(n_b, 1, Q_LORA)
    b_gkv = b_g_kv.reshape(n_b, 1, KV_LORA)
    b_wo = b_w_o.astype(BF16)

    for i in range(DEPTH):
        j = i // 2
        if i % 2 == 0:
            q, kcat, vt = _norm_qkv(h, g_mix, i, a_wqk, a_wvt, j)
            attn, w_gu, w_down = _win_attn(q.reshape(B, S, nq), kcat, vt, pos_col3, pos_blk3,
                                           a_sink[j], ffn_w_gu, ffn_w_down, i)
            attn = attn.reshape(T, nq)
            w_o = a_wo
        else:
            qh, kh, v = _mla_prep(h.reshape(B, S, D), g_mix, i, b_win, b_gq, b_gkv,
                                  b_wuq, b_wuk, b_wuvt, j, ct, sg)
            attn, w_gu, w_down = _mla_attn(qh, kh, v, ffn_w_gu, ffn_w_down, i)
            attn = attn.reshape(T, B_HEADS * V_DIM)
            w_o = b_wo
        h = _proj_ffn(h, attn, w_o, j, g_ffn, w_gu, w_down, i, gf,
                      final_norm=(i == DEPTH - 1))
    return h.reshape(B, S, D)
```
